```python
import jax
import jax.numpy as jnp
from jax import lax
import numpy as np


D_MODEL = 1024
BATCH = 16
SEQ = 2048
DEPTH = 2

N_EVEN = (DEPTH + 1) // 2
N_ODD = DEPTH // 2
CONV_WIDTH = 4
EPS = 1e-6
SSD_HEADS = 16
SSD_HEAD_DIM = 64
SSD_INNER = SSD_HEADS * SSD_HEAD_DIM
SSD_GROUPS = 2
SSD_STATE = 128
SSD_CHUNK = 128
SSD_CONV_CH = SSD_INNER + 2 * SSD_GROUPS * SSD_STATE
GDN_HEADS = 8
GDN_DK = 128
GDN_DV = 128
GDN_CHUNK = 64
GDN_QK = GDN_HEADS * GDN_DK
GDN_VAL = GDN_HEADS * GDN_DV
GDN_CONV_CH = 2 * GDN_QK + GDN_VAL
A_SPLIT_SIZES = (SSD_INNER, SSD_CONV_CH, SSD_HEADS, GDN_CONV_CH, GDN_VAL, GDN_HEADS, GDN_HEADS)
A_IN_WIDTH = sum(A_SPLIT_SIZES)
A_OUT_WIDTH = SSD_INNER + GDN_VAL
SB_HEADS = 16
SB_HEAD_DIM = D_MODEL // SB_HEADS
SB_BLOCK = 128
D_FF = 4 * D_MODEL

kernel_name = 'hybrid_ssd_gdn_stickbreak_block'


def rmsnorm(x, w):
    xf = x.astype(jnp.float32)
    y = xf * lax.rsqrt(jnp.mean(xf * xf, axis=-1, keepdims=True) + EPS)
    return (y * w.astype(jnp.float32)).astype(x.dtype)


def l2norm(x):
    xf = x.astype(jnp.float32)
    return xf * lax.rsqrt(jnp.sum(xf * xf, axis=-1, keepdims=True) + EPS)


def split_cols(t, sizes):
    offs = np.cumsum(np.array(sizes))[:-1].tolist()
    return jnp.split(t, offs, axis=-1)


def causal_conv(x, w):
    k = w.shape[0]
    return lax.conv_general_dilated(
        x, w[:, None, :].astype(x.dtype), window_strides=(1,), padding=[(k - 1, 0)],
        dimension_numbers=('NWC', 'WIO', 'NWC'), feature_group_count=x.shape[-1])


def ssd_mixer(z, xbc, dt_raw, conv_w, conv_b, dt_bias, a_log, d_skip, norm_w):
    bsz, seqlen, _ = z.shape
    nc = seqlen // SSD_CHUNK
    hpg = SSD_HEADS // SSD_GROUPS
    xbc = jax.nn.silu(causal_conv(xbc, conv_w) + conv_b.astype(xbc.dtype)).astype(jnp.float32)
    xs, bm, cm = split_cols(xbc, (SSD_INNER, SSD_GROUPS * SSD_STATE, SSD_GROUPS * SSD_STATE))
    xs = xs.reshape(bsz, nc, SSD_CHUNK, SSD_GROUPS, hpg, SSD_HEAD_DIM)
    bm = bm.reshape(bsz, nc, SSD_CHUNK, SSD_GROUPS, SSD_STATE)
    cm = cm.reshape(bsz, nc, SSD_CHUNK, SSD_GROUPS, SSD_STATE)
    dt = jax.nn.softplus(dt_raw.astype(jnp.float32) + dt_bias.astype(jnp.float32))
    dt = dt.reshape(bsz, nc, SSD_CHUNK, SSD_GROUPS, hpg)
    a = dt * (-jnp.exp(a_log.astype(jnp.float32))).reshape(SSD_GROUPS, hpg)
    a_cum = jnp.cumsum(a, axis=2)
    xdt = xs * dt[..., None]
    causal = jnp.tril(jnp.ones((SSD_CHUNK, SSD_CHUNK), dtype=bool))
    seg = a_cum[:, :, :, None] - a_cum[:, :, None, :]
    lmat = jnp.exp(jnp.where(causal[:, :, None, None], seg, -jnp.inf))
    cb = jnp.einsum('bclgn,bcsgn->bclsg', cm, bm)
    y_diag = jnp.einsum('bclsg,bclsgh,bcsghp->bclghp', cb, lmat, xdt)
    decay_to_end = jnp.exp(a_cum[:, :, -1:] - a_cum)
    states = jnp.einsum('bclgn,bclgh,bclghp->bcghpn', bm, decay_to_end, xdt)
    chunk_decay = jnp.exp(a_cum[:, :, -1])

    def step(s, inp):
        st, dec = inp
        return s * dec[..., None, None] + st, s

    init = jnp.zeros((bsz, SSD_GROUPS, hpg, SSD_HEAD_DIM, SSD_STATE), jnp.float32)
    _, prev = lax.scan(step, init, (jnp.moveaxis(states, 1, 0), jnp.moveaxis(chunk_decay, 1, 0)))
    prev = jnp.moveaxis(prev, 0, 1)
    y_off = jnp.einsum('bclgn,bcghpn,bclgh->bclghp', cm, prev, jnp.exp(a_cum))
    y = y_diag + y_off + xs * d_skip.astype(jnp.float32).reshape(SSD_GROUPS, hpg)[:, :, None]
    y = y.reshape(bsz, seqlen, SSD_INNER) * jax.nn.silu(z.astype(jnp.float32))
    y = y.reshape(bsz, seqlen, SSD_GROUPS, SSD_INNER // SSD_GROUPS)
    y = y * lax.rsqrt(jnp.mean(y * y, axis=-1, keepdims=True) + EPS)
    return y.reshape(bsz, seqlen, SSD_INNER) * norm_w.astype(jnp.float32)


def gdn_mixer(qkv, zg, b_raw, a_raw, conv_w, a_log, dt_bias, norm_w):
    bsz, seqlen, _ = qkv.shape
    c = GDN_CHUNK
    nc = seqlen // c
    qkv = jax.nn.silu(causal_conv(qkv, conv_w))
    q, k, v = split_cols(qkv, (GDN_QK, GDN_QK, GDN_VAL))
    q = l2norm(q.reshape(bsz, seqlen, GDN_HEADS, GDN_DK)) * (GDN_DK ** -0.5)
    k = l2norm(k.reshape(bsz, seqlen, GDN_HEADS, GDN_DK))
    v = v.reshape(bsz, seqlen, GDN_HEADS, GDN_DV).astype(jnp.float32)
    beta = jax.nn.sigmoid(b_raw.astype(jnp.float32))
    g = -jnp.exp(a_log.astype(jnp.float32)) * jax.nn.softplus(a_raw.astype(jnp.float32) + dt_bias.astype(jnp.float32))

    def chunks(t):
        return t.reshape(bsz, nc, c, GDN_HEADS, -1).transpose(0, 3, 1, 2, 4)

    q, k, v = chunks(q), chunks(k), chunks(v)
    beta = beta.reshape(bsz, nc, c, GDN_HEADS).transpose(0, 3, 1, 2)
    gc = jnp.cumsum(g.reshape(bsz, nc, c, GDN_HEADS).transpose(0, 3, 1, 2), axis=-1)
    causal = jnp.tril(jnp.ones((c, c), dtype=bool))
    strict = jnp.tril(jnp.ones((c, c), dtype=bool), k=-1)
    decay_mat = jnp.exp(jnp.where(causal, gc[..., :, None] - gc[..., None, :], -jnp.inf))
    kk = jnp.einsum('bhnld,bhnsd->bhnls', k, k)
    amat = jnp.where(strict, kk * beta[..., None] * decay_mat, 0.0)
    eye = jnp.eye(c, dtype=jnp.float32)
    tmat = lax.linalg.triangular_solve(eye + amat, jnp.broadcast_to(eye, amat.shape),
                                       left_side=True, lower=True, unit_diagonal=True)
    u_base = jnp.einsum('bhnls,bhnsd->bhnld', tmat, v * beta[..., None])
    w = jnp.einsum('bhnls,bhnsd->bhnld', tmat, k * (beta * jnp.exp(gc))[..., None])
    qk = jnp.where(causal, jnp.einsum('bhnld,bhnsd->bhnls', q, k) * decay_mat, 0.0)
    q_dec = q * jnp.exp(gc)[..., None]
    k_dec = k * jnp.exp(gc[..., -1:] - gc)[..., None]
    chunk_decay = jnp.exp(gc[..., -1])

    def step(s, inp):
        u_b, w_c, q_d, k_d, qk_c, dec = inp
        u = u_b - jnp.einsum('bhld,bhde->bhle', w_c, s)
        o = jnp.einsum('bhld,bhde->bhle', q_d, s) + jnp.einsum('bhls,bhse->bhle', qk_c, u)
        s = s * dec[..., None, None] + jnp.einsum('bhld,bhle->bhde', k_d, u)
        return s, o

    seq_in = tuple(jnp.moveaxis(t, 2, 0) for t in (u_base, w, q_dec, k_dec, qk, chunk_decay))
    s0 = jnp.zeros((bsz, GDN_HEADS, GDN_DK, GDN_DV), jnp.float32)
    _, o = lax.scan(step, s0, seq_in)
    o = o.transpose(1, 0, 3, 2, 4).reshape(bsz, seqlen, GDN_HEADS, GDN_DV)
    o = rmsnorm(o, norm_w) * jax.nn.silu(zg.astype(jnp.float32).reshape(bsz, seqlen, GDN_HEADS, GDN_DV))
    return o.reshape(bsz, seqlen, GDN_VAL)


def stick_breaking_attention(h, w_qkv, q_norm_w, k_norm_w, w_o):
    bsz, seqlen, _ = h.shape
    q, k, v = split_cols(h @ w_qkv, (D_MODEL, D_MODEL, D_MODEL))
    q = rmsnorm(q.reshape(bsz, seqlen, SB_HEADS, SB_HEAD_DIM), q_norm_w).transpose(0, 2, 1, 3)
    k = rmsnorm(k.reshape(bsz, seqlen, SB_HEADS, SB_HEAD_DIM), k_norm_w).transpose(0, 2, 1, 3)
    v = v.reshape(bsz, seqlen, SB_HEADS, SB_HEAD_DIM).transpose(0, 2, 1, 3).astype(jnp.float32)
    scale = SB_HEAD_DIM ** -0.5
    outs = []
    for blk in range(seqlen // SB_BLOCK):
        q0 = blk * SB_BLOCK
        kend = q0 + SB_BLOCK
        logits = jnp.einsum('bhtd,bhsd->bhts', q[:, :, q0:kend], k[:, :, :kend]).astype(jnp.float32) * scale
        t_pos = q0 + jnp.arange(SB_BLOCK)[:, None]
        s_pos = jnp.arange(kend)[None, :]
        valid = s_pos < t_pos
        log_beta = jax.nn.log_sigmoid(logits)
        log_1m = jnp.where(valid, jax.nn.log_sigmoid(-logits), 0.0)
        suffix = lax.cumsum(log_1m, axis=3, reverse=True) - log_1m
        att = jnp.where(valid, jnp.exp(log_beta + suffix), 0.0)
        outs.append(jnp.einsum('bhts,bhsd->bhtd', att, v[:, :, :kend]))
    o = jnp.concatenate(outs, axis=2).transpose(0, 2, 1, 3).reshape(bsz, seqlen, D_MODEL)
    return o.astype(h.dtype) @ w_o


def sqrelu_mlp(h, w1, w2):
    a = jax.nn.relu(h @ w1)
    return (a * a) @ w2


def _normal(k, shape, scale):
    return jax.random.normal(k, shape, jnp.float32) * scale


def _gain(k, shape):
    return 1.0 + 0.02 * jax.random.normal(k, shape, jnp.float32)


def _dt_bias(k, shape):
    dt = jnp.exp(jax.random.uniform(k, shape, jnp.float32) * (jnp.log(0.1) - jnp.log(0.001)) + jnp.log(0.001))
    return dt + jnp.log(-jnp.expm1(-dt))


def _a_log(k, shape):
    return jnp.log(jax.random.uniform(k, shape, jnp.float32, 1.0, 16.0))


def setup_inputs(seed: int = 0) -> dict:
    key = jax.random.key(seed)
    ks = jax.random.split(key, 24)
    return {
        'x': jax.random.normal(ks[0], (BATCH, SEQ, D_MODEL), jnp.float32),
        'a_norm_w': _gain(ks[1], (N_EVEN, D_MODEL)),
        'a_w_in': _normal(ks[2], (N_EVEN, D_MODEL, A_IN_WIDTH), D_MODEL ** -0.5),
        'ssd_conv_w': _normal(ks[3], (N_EVEN, CONV_WIDTH, SSD_CONV_CH), CONV_WIDTH ** -0.5),
        'ssd_conv_b': _normal(ks[4], (N_EVEN, SSD_CONV_CH), 0.02),
        'ssd_dt_bias': _dt_bias(ks[5], (N_EVEN, SSD_HEADS)),
        'ssd_a_log': _a_log(ks[6], (N_EVEN, SSD_HEADS)),
        'ssd_d_skip': _gain(ks[7], (N_EVEN, SSD_HEADS)),
        'ssd_norm_w': _gain(ks[8], (N_EVEN, SSD_INNER)),
        'gdn_conv_w': _normal(ks[9], (N_EVEN, CONV_WIDTH, GDN_CONV_CH), CONV_WIDTH ** -0.5),
        'gdn_a_log': _a_log(ks[10], (N_EVEN, GDN_HEADS)),
        'gdn_dt_bias': _dt_bias(ks[11], (N_EVEN, GDN_HEADS)),
        'gdn_norm_w': _gain(ks[12], (N_EVEN, GDN_DV)),
        'a_w_out': _normal(ks[13], (N_EVEN, A_OUT_WIDTH, D_MODEL), A_OUT_WIDTH ** -0.5),
        'c_norm_w': _gain(ks[14], (N_ODD, D_MODEL)),
        'c_w_qkv': _normal(ks[15], (N_ODD, D_MODEL, 3 * D_MODEL), D_MODEL ** -0.5),
        'c_q_norm_w': _gain(ks[16], (N_ODD, SB_HEAD_DIM)),
        'c_k_norm_w': _gain(ks[17], (N_ODD, SB_HEAD_DIM)),
        'c_w_o': _normal(ks[18], (N_ODD, D_MODEL, D_MODEL), D_MODEL ** -0.5),
        'mlp_norm_w': _gain(ks[19], (DEPTH, D_MODEL)),
        'mlp_w1': _normal(ks[20], (DEPTH, D_MODEL, D_FF), D_MODEL ** -0.5),
        'mlp_w2': _normal(ks[21], (DEPTH, D_FF, D_MODEL), D_FF ** -0.5),
    }


def reference(x, a_norm_w, a_w_in, ssd_conv_w, ssd_conv_b, ssd_dt_bias, ssd_a_log, ssd_d_skip,
              ssd_norm_w, gdn_conv_w, gdn_a_log, gdn_dt_bias, gdn_norm_w, a_w_out,
              c_norm_w, c_w_qkv, c_q_norm_w, c_k_norm_w, c_w_o,
              mlp_norm_w, mlp_w1, mlp_w2):
    for layer in range(DEPTH):
        i = layer // 2
        if layer % 2 == 0:
            h = rmsnorm(x, a_norm_w[i])
            proj = h @ a_w_in[i]
            s_z, s_xbc, s_dt, g_qkv, g_z, g_b, g_a = split_cols(proj, A_SPLIT_SIZES)
            y_ssd = ssd_mixer(s_z, s_xbc, s_dt, ssd_conv_w[i], ssd_conv_b[i], ssd_dt_bias[i],
                              ssd_a_log[i], ssd_d_skip[i], ssd_norm_w[i])
            y_gdn = gdn_mixer(g_qkv, g_z, g_b, g_a, gdn_conv_w[i], gdn_a_log[i], gdn_dt_bias[i], gdn_norm_w[i])
            mix = jnp.concatenate([y_ssd, y_gdn], axis=-1).astype(x.dtype) @ a_w_out[i]
        else:
            h = rmsnorm(x, c_norm_w[i])
            mix = stick_breaking_attention(h, c_w_qkv[i], c_q_norm_w[i], c_k_norm_w[i], c_w_o[i])
        x = x + mix.astype(x.dtype)
        x = x + sqrelu_mlp(rmsnorm(x, mlp_norm_w[layer]), mlp_w1[layer], mlp_w2[layer]).astype(x.dtype)
    return x
```

```python
import functools

import jax
import jax.numpy as jnp
from jax import lax
from jax.experimental import pallas as pl
from jax.experimental.pallas import tpu as pltpu

F32 = jnp.float32
BF16 = jnp.bfloat16

EPS = 1e-6
D_MODEL = 1024
D_FF = 4 * D_MODEL
CONV_WIDTH = 4
SSD_HEADS = 16
SSD_HEAD_DIM = 64
SSD_INNER = SSD_HEADS * SSD_HEAD_DIM
SSD_GROUPS = 2
SSD_STATE = 128
SSD_CHUNK = 128
GDN_HEADS = 8
GDN_DK = 128
GDN_DV = 128
GDN_CHUNK = 64
GDN_QK = GDN_HEADS * GDN_DK
GDN_VAL = GDN_HEADS * GDN_DV
SB_HEADS = 16
SB_HEAD_DIM = D_MODEL // SB_HEADS
SB_BLOCK = 128

LANES = 128
SUBLANES = 8
VMEM_LIMIT_BYTES = 56 * 1024 * 1024

SM_DT = 0
SM_BETA = SSD_HEADS
SM_ALPHA = SSD_HEADS + GDN_HEADS
MAIN_WIDTH = 3 * GDN_QK + SSD_INNER + GDN_VAL + SSD_INNER + 2 * SSD_GROUPS * SSD_STATE
COL_Q, COL_K, COL_V, COL_Z, COL_GZ, COL_XS = 0, 1, 2, 3, 4, 5
COL_BC = (3 * GDN_QK + SSD_INNER + GDN_VAL + SSD_INNER) // (2 * SSD_GROUPS * SSD_STATE)

NEG_BIG = -1e30


def _dot(a, b):
    return jnp.dot(a, b, preferred_element_type=F32)


def _dot_nt(a, b):
    return lax.dot_general(a, b, (((1,), (1,)), ((), ())), preferred_element_type=F32)


def _dot_tn(a, b):
    return lax.dot_general(a, b, (((0,), (0,)), ((), ())), preferred_element_type=F32)


def _split_hi_lo(v):
    hi = v.astype(BF16)
    lo = (v - hi.astype(F32)).astype(BF16)
    return hi, lo


def _softplus(x):
    return jnp.maximum(x, 0.0) + jnp.log1p(jnp.exp(-jnp.abs(x)))


def _sigmoid(x):
    return 1.0 / (1.0 + jnp.exp(-x))


def _silu(x):
    return x * _sigmoid(x)


def _lower_tri(n, dtype):
    row = lax.broadcasted_iota(jnp.int32, (n, n), 0)
    col = lax.broadcasted_iota(jnp.int32, (n, n), 1)
    return (row >= col).astype(dtype)


def _conv_silu(x, tail_ref, w_ref, b_ref):
    t, c = x.shape
    tail = tail_ref[...]
    acc = x * w_ref[CONV_WIDTH - 1:CONV_WIDTH, :]
    if b_ref is not None:
        acc = acc + b_ref[...]
    row8 = lax.broadcasted_iota(jnp.int32, (SUBLANES, c), 0)
    for k in range(1, CONV_WIDTH):
        r = pltpu.roll(x, k, 0)
        tr = pltpu.roll(tail, k, 0)
        first = jnp.where(row8 < k, tr, r[0:SUBLANES])
        shifted = jnp.concatenate([first, r[SUBLANES:]], axis=0)
        acc = acc + shifted * w_ref[CONV_WIDTH - 1 - k:CONV_WIDTH - k, :]
    tail_ref[...] = x[t - SUBLANES:t]
    return _silu(acc)


PROJ_COL_CHUNK = 512


def _rmsnorm_rows(x, w):
    ms = jnp.mean(x * x, axis=-1, keepdims=True)
    return x * lax.rsqrt(ms + EPS) * w


def _in_proj_kernel(x_ref, nw_ref, w_ref, ws_ref, main_ref, small_ref):
    h = _rmsnorm_rows(x_ref[...], nw_ref[...]).astype(BF16)
    small_ref[...] = _dot(h, ws_ref[...])
    for c in range(MAIN_WIDTH // PROJ_COL_CHUNK):
        sl = slice(c * PROJ_COL_CHUNK, (c + 1) * PROJ_COL_CHUNK)
        main_ref[:, sl] = _dot(h, w_ref[:, sl]).astype(main_ref.dtype)


def _const_spec(shape):
    nd = len(shape)
    return pl.BlockSpec(shape, lambda *_: (0,) * nd, pipeline_mode=pl.Buffered(1))


def _in_proj(x2, norm_w, w_main, w_small, tm=256):
    m = x2.shape[0]
    return pl.pallas_call(
        _in_proj_kernel,
        grid=(m // tm,),
        in_specs=[
            pl.BlockSpec((tm, D_MODEL), lambda i: (i, 0)),
            _const_spec((1, D_MODEL)),
            _const_spec((D_MODEL, MAIN_WIDTH)),
            _const_spec((D_MODEL, LANES)),
        ],
        out_specs=[
            pl.BlockSpec((tm, MAIN_WIDTH), lambda i: (i, 0)),
            pl.BlockSpec((tm, LANES), lambda i: (i, 0)),
        ],
        out_shape=[
            jax.ShapeDtypeStruct((m, MAIN_WIDTH), F32),
            jax.ShapeDtypeStruct((m, LANES), F32),
        ],
        compiler_params=pltpu.CompilerParams(
            dimension_semantics=("parallel",), vmem_limit_bytes=VMEM_LIMIT_BYTES),
        name="in_proj",
    )(x2, norm_w, w_main, w_small)


def _qkv_proj_kernel(x_ref, nw_ref, w_ref, gmat_ref, qw_ref, kw_ref, o_ref):
    h = _rmsnorm_rows(x_ref[...], nw_ref[...]).astype(BF16)
    gmat = gmat_ref[...]
    n_chunks = D_MODEL // PROJ_COL_CHUNK
    for part, hw_ref in ((0, qw_ref), (1, kw_ref)):
        for c in range(n_chunks):
            col = part * D_MODEL + c * PROJ_COL_CHUNK
            y = _dot(h, w_ref[:, col:col + PROJ_COL_CHUNK])
            ms = _dot((y * y).astype(BF16), gmat) * (1.0 / SB_HEAD_DIM)
            o_ref[:, col:col + PROJ_COL_CHUNK] = (y * lax.rsqrt(ms + EPS) * hw_ref[...]).astype(o_ref.dtype)
    for c in range(n_chunks):
        col = 2 * D_MODEL + c * PROJ_COL_CHUNK
        o_ref[:, col:col + PROJ_COL_CHUNK] = _dot(h, w_ref[:, col:col + PROJ_COL_CHUNK]).astype(o_ref.dtype)


def _qkv_proj(x2, norm_w, w_qkv, gmat, qw, kw, tm=512):
    m = x2.shape[0]
    return pl.pallas_call(
        _qkv_proj_kernel,
        grid=(m // tm,),
        in_specs=[
            pl.BlockSpec((tm, D_MODEL), lambda i: (i, 0)),
            _const_spec((1, D_MODEL)),
            _const_spec((D_MODEL, 3 * D_MODEL)),
            _const_spec((PROJ_COL_CHUNK, PROJ_COL_CHUNK)),
            _const_spec((1, PROJ_COL_CHUNK)),
            _const_spec((1, PROJ_COL_CHUNK)),
        ],
        out_specs=pl.BlockSpec((tm, 3 * D_MODEL), lambda i: (i, 0)),
        out_shape=jax.ShapeDtypeStruct((m, 3 * D_MODEL), BF16),
        compiler_params=pltpu.CompilerParams(
            dimension_semantics=("parallel",), vmem_limit_bytes=VMEM_LIMIT_BYTES),
        name="qkv_proj",
    )(x2, norm_w, w_qkv, gmat, qw, kw)


def _ssd_kernel(z_ref, xs_ref, bc_ref, sm_ref, cwx_ref, cbx_ref, cwb_ref, cbb_ref,
                dtb_ref, alog_ref, dsk_ref, nw_ref, e_ref,
                y_ref, tailx_ref, tailb_ref, state_ref):
    t = SSD_CHUNK
    n = SSD_STATE
    hpg = SSD_HEADS // SSD_GROUPS
    gw = hpg * SSD_HEAD_DIM

    @pl.when(pl.program_id(1) == 0)
    def _init():
        tailx_ref[...] = jnp.zeros_like(tailx_ref)
        tailb_ref[...] = jnp.zeros_like(tailb_ref)
        state_ref[...] = jnp.zeros_like(state_ref)

    xs = _conv_silu(xs_ref[...], tailx_ref, cwx_ref, cbx_ref)
    bc = _conv_silu(bc_ref[...], tailb_ref, cwb_ref, cbb_ref)

    lane = lax.broadcasted_iota(jnp.int32, (1, LANES), 1)
    head_lane = (lane >= SM_DT) & (lane < SM_DT + SSD_HEADS)
    dt = jnp.where(head_lane, _softplus(sm_ref[...] + dtb_ref[...]), 0.0)
    a = dt * (-jnp.exp(alog_ref[...]))
    tri = _lower_tri(t, BF16)
    a_hi, a_lo = _split_hi_lo(a)
    a_cum = _dot(tri, a_hi) + _dot(tri, a_lo)
    a_cum_t = a_cum.T
    dt_t = dt.T
    total = a_cum[t - 1:t, :]
    e_a = jnp.exp(a_cum)
    w_state = dt * jnp.exp(total - a_cum)

    ex_hi, ex_lo = _split_hi_lo(jnp.concatenate([w_state, e_a], axis=0))
    e_mat = e_ref[...]
    ex = _dot(ex_hi, e_mat) + _dot(ex_lo, e_mat)
    w_state_x = ex[:t]
    e_a_x = ex[t:]
    chunk_decay_x = e_a_x[t - 1:t, :]

    xw = (xs * w_state_x).astype(BF16)
    state_prev = state_ref[...]
    state_prev_b = state_prev.astype(BF16)
    upd, y_off, cb = [], [], []
    for g in range(SSD_GROUPS):
        b_g = bc[:, g * n:(g + 1) * n].astype(BF16)
        c_g = bc[:, (SSD_GROUPS + g) * n:(SSD_GROUPS + g + 1) * n].astype(BF16)
        upd.append(_dot_tn(b_g, xw[:, g * gw:(g + 1) * gw]))
        y_off.append(_dot(c_g, state_prev_b[:, g * gw:(g + 1) * gw]))
        cb.append(_dot_nt(c_g, b_g))
    state_ref[...] = state_prev * chunk_decay_x + jnp.concatenate(upd, axis=1)
    y_off = jnp.concatenate(y_off, axis=1) * e_a_x

    quad = 4
    lane_c = lax.broadcasted_iota(jnp.int32, (1, SSD_INNER), 1)
    head_in_quad = (lane_c // SSD_HEAD_DIM) % quad
    xm = [jnp.where(head_in_quad == j, xs, 0.0).astype(BF16) for j in range(quad)]
    row = lax.broadcasted_iota(jnp.int32, (t, t), 0)
    col = lax.broadcasted_iota(jnp.int32, (t, t), 1)
    causal = row >= col
    qw = quad * SSD_HEAD_DIM
    y_quads = [None] * (SSD_HEADS // quad)
    for h in range(SSD_HEADS):
        g, q, j = h // hpg, h // quad, h % quad
        a_col = jnp.broadcast_to(a_cum[:, h:h + 1], (t, t))
        seg = jnp.where(causal, a_col - a_cum_t[h:h + 1, :], NEG_BIG)
        m = (cb[g] * jnp.exp(seg) * dt_t[h:h + 1, :]).astype(BF16)
        part = _dot(m, xm[j][:, q * qw:(q + 1) * qw])
        y_quads[q] = part if y_quads[q] is None else y_quads[q] + part
    y = jnp.concatenate(y_quads, axis=1) + y_off + xs * dsk_ref[...]

    y = y * _silu(z_ref[...])
    parts = []
    for g in range(SSD_GROUPS):
        yg = y[:, g * gw:(g + 1) * gw]
        ms = jnp.mean(yg * yg, axis=-1, keepdims=True)
        parts.append(yg * lax.rsqrt(ms + EPS))
    y_ref[...] = (jnp.concatenate(parts, axis=1) * nw_ref[...]).astype(y_ref.dtype)


def _ssd_mixer(main, small, bsz, seqlen, cwx, cbx, cwb, cbb, dtb, alog, dsk, nw, e_mat):
    t = SSD_CHUNK
    nc = seqlen // t
    bc_w = 2 * SSD_GROUPS * SSD_STATE
    return pl.pallas_call(
        _ssd_kernel,
        grid=(bsz, nc),
        in_specs=[
            pl.BlockSpec((t, SSD_INNER), lambda b, c: (b * nc + c, COL_Z)),
            pl.BlockSpec((t, SSD_INNER), lambda b, c: (b * nc + c, COL_XS)),
            pl.BlockSpec((t, bc_w), lambda b, c: (b * nc + c, COL_BC)),
            pl.BlockSpec((t, LANES), lambda b, c: (b * nc + c, 0)),
            _const_spec((CONV_WIDTH, SSD_INNER)),
            _const_spec((1, SSD_INNER)),
            _const_spec((CONV_WIDTH, bc_w)),
            _const_spec((1, bc_w)),
            _const_spec((1, LANES)),
            _const_spec((1, LANES)),
            _const_spec((1, SSD_INNER)),
            _const_spec((1, SSD_INNER)),
            _const_spec((LANES, SSD_INNER)),
        ],
        out_specs=pl.BlockSpec((t, SSD_INNER), lambda b, c: (b * nc + c, 0)),
        out_shape=jax.ShapeDtypeStruct((bsz * seqlen, SSD_INNER), BF16),
        scratch_shapes=[
            pltpu.VMEM((SUBLANES, SSD_INNER), F32),
            pltpu.VMEM((SUBLANES, bc_w), F32),
            pltpu.VMEM((SSD_STATE, SSD_INNER), F32),
        ],
        compiler_params=pltpu.CompilerParams(
            dimension_semantics=("parallel", "arbitrary"), vmem_limit_bytes=VMEM_LIMIT_BYTES),
        name="ssd_mixer",
    )(main, main, main, small, cwx, cbx, cwb, cbb, dtb, alog, dsk, nw, e_mat)


def _l2norm_rows(x):
    return x * lax.rsqrt(jnp.sum(x * x, axis=-1, keepdims=True) + EPS)


def _gdn_kernel(q_ref, k_ref, v_ref, gz_ref, sm_ref, cwq_ref, cwk_ref, cwv_ref,
                dtb_ref, alog_ref, nw_ref, sel_ref,
                o_ref, tq_ref, tk_ref, tv_ref, s_ref):
    t = GDN_CHUNK

    @pl.when(pl.program_id(1) == 0)
    def _init():
        tq_ref[...] = jnp.zeros_like(tq_ref)
        tk_ref[...] = jnp.zeros_like(tk_ref)
        tv_ref[...] = jnp.zeros_like(tv_ref)
        s_ref[...] = jnp.zeros_like(s_ref)

    q = _conv_silu(q_ref[...], tq_ref, cwq_ref, None)
    k = _conv_silu(k_ref[...], tk_ref, cwk_ref, None)
    v = _conv_silu(v_ref[...], tv_ref, cwv_ref, None)

    sm = sm_ref[...]
    lane = lax.broadcasted_iota(jnp.int32, (1, LANES), 1)
    alpha_lane = (lane >= SM_ALPHA) & (lane < SM_ALPHA + GDN_HEADS)
    beta = _sigmoid(sm)
    g = jnp.where(alpha_lane, -jnp.exp(alog_ref[...]) * _softplus(sm + dtb_ref[...]), 0.0)
    tri = _lower_tri(t, BF16)
    g_hi, g_lo = _split_hi_lo(g)
    gc = _dot(tri, g_hi) + _dot(tri, g_lo)
    gc_hi, gc_lo = _split_hi_lo(gc)
    sel = sel_ref[...]
    gc_rows = _dot_nt(sel, gc_hi) + _dot_nt(sel, gc_lo)
    total = gc[t - 1:t, :]
    e_gc = jnp.exp(gc)
    e_k = jnp.exp(total - gc)

    row = lax.broadcasted_iota(jnp.int32, (t, t), 0)
    col = lax.broadcasted_iota(jnp.int32, (t, t), 1)
    causal = row >= col
    strict = row > col
    eye = (row == col).astype(F32)
    nw = nw_ref[...]
    gz = gz_ref[...]

    for h in range(GDN_HEADS):
        sl = slice(h * GDN_DK, (h + 1) * GDN_DK)
        qh = _l2norm_rows(q[:, sl]) * (GDN_DK ** -0.5)
        kh = _l2norm_rows(k[:, sl])
        vh = v[:, sl]
        lb, la = SM_BETA + h, SM_ALPHA + h
        beta_c = jnp.broadcast_to(beta[:, lb:lb + 1], (t, GDN_DK))
        gc_c = jnp.broadcast_to(gc[:, la:la + 1], (t, GDN_DK))
        e_gc_c = jnp.broadcast_to(e_gc[:, la:la + 1], (t, GDN_DK))
        e_k_c = jnp.broadcast_to(e_k[:, la:la + 1], (t, GDN_DK))
        decay = jnp.exp(jnp.where(causal, gc_c[:, :t] - gc_rows[h:h + 1, :], NEG_BIG))

        kb = kh.astype(BF16)
        qb = qh.astype(BF16)
        amat = jnp.where(strict, _dot_nt(kb, kb) * beta_c[:, :t] * decay, 0.0)
        amat_b = amat.astype(BF16)
        x = eye - amat
        p = _dot(amat_b, amat_b)
        n_sq = (t - 1).bit_length() - 1
        for it in range(n_sq):
            p_b = p.astype(BF16)
            x = x + _dot(x.astype(BF16), p_b)
            if it + 1 < n_sq:
                p = _dot(p_b, p_b)
        rhs = jnp.concatenate([vh * beta_c, kh * (beta_c * e_gc_c)], axis=1).astype(BF16)
        uw = _dot(x.astype(BF16), rhs)
        u_base = uw[:, :GDN_DV]
        w = uw[:, GDN_DV:]
        qk = jnp.where(causal, _dot_nt(qb, kb) * decay, 0.0)
        q_dec = qh * e_gc_c
        k_dec = kh * e_k_c

        s_prev = s_ref[h]
        wq_s = _dot(jnp.concatenate([w, q_dec], axis=0).astype(BF16), s_prev.astype(BF16))
        u = u_base - wq_s[:t]
        u_b = u.astype(BF16)
        o = wq_s[t:] + _dot(qk.astype(BF16), u_b)
        s_ref[h] = s_prev * e_gc_c[t - 1:t, :] + _dot_tn(k_dec.astype(BF16), u_b)

        o = o * lax.rsqrt(jnp.mean(o * o, axis=-1, keepdims=True) + EPS) * nw
        o_ref[:, sl] = (o * _silu(gz[:, sl])).astype(o_ref.dtype)


def _gdn_mixer(main, small, bsz, seqlen, cw, dtb, alog, nw, sel):
    t = GDN_CHUNK
    nc = seqlen // t
    row_spec = lambda colblk: pl.BlockSpec((t, GDN_QK), lambda b, c: (b * nc + c, colblk))
    cw_spec = lambda colblk: pl.BlockSpec((CONV_WIDTH, GDN_QK), lambda b, c: (0, colblk),
                                          pipeline_mode=pl.Buffered(1))
    return pl.pallas_call(
        _gdn_kernel,
        grid=(bsz, nc),
        in_specs=[
            row_spec(COL_Q), row_spec(COL_K), row_spec(COL_V), row_spec(COL_GZ),
            pl.BlockSpec((t, LANES), lambda b, c: (b * nc + c, 0)),
            cw_spec(0), cw_spec(1), cw_spec(2),
            _const_spec((1, LANES)),
            _const_spec((1, LANES)),
            _const_spec((1, GDN_DV)),
            _const_spec((SUBLANES, LANES)),
        ],
        out_specs=pl.BlockSpec((t, GDN_VAL), lambda b, c: (b * nc + c, 0)),
        out_shape=jax.ShapeDtypeStruct((bsz * seqlen, GDN_VAL), BF16),
        scratch_shapes=[
            pltpu.VMEM((SUBLANES, GDN_QK), F32),
            pltpu.VMEM((SUBLANES, GDN_QK), F32),
            pltpu.VMEM((SUBLANES, GDN_VAL), F32),
            pltpu.VMEM((GDN_HEADS, GDN_DK, GDN_DV), F32),
        ],
        compiler_params=pltpu.CompilerParams(
            dimension_semantics=("parallel", "arbitrary"), vmem_limit_bytes=VMEM_LIMIT_BYTES),
        name="gdn_mixer",
    )(main, main, main, main, small, cw, cw, cw, dtb, alog, nw, sel)


def _sb_kernel(q_ref, k_ref, v_ref, suo_ref, o_ref, acc_ref, carry_ref):
    t = SB_BLOCK
    qi = pl.program_id(2)
    heads_per_tile = LANES // SB_HEAD_DIM
    lane = lax.broadcasted_iota(jnp.int32, (1, LANES), 1)
    head_masks = [(lane // SB_HEAD_DIM) == p for p in range(heads_per_tile)]
    q = q_ref[...]
    zero_b = jnp.zeros((), BF16)
    qm = [jnp.where(hm, q, zero_b) for hm in head_masks]
    row = lax.broadcasted_iota(jnp.int32, (t, t), 0)
    col = lax.broadcasted_iota(jnp.int32, (t, t), 1)
    valid = col < row
    suo = suo_ref[...]

    acc_ref[...] = jnp.zeros_like(acc_ref)
    carry_ref[...] = jnp.zeros_like(carry_ref)

    def key_block(j, diagonal):
        start = pl.multiple_of(j * t, t)
        kb = k_ref[pl.ds(start, t), :]
        vb = v_ref[pl.ds(start, t), :]
        out = None
        for p in range(heads_per_tile):
            s = _dot_nt(qm[p], kb)
            soft = jnp.log1p(jnp.exp(-jnp.abs(s)))
            log_1m = -(jnp.maximum(s, 0.0) + soft)
            log_beta = log_1m + s
            if diagonal:
                log_1m = jnp.where(valid, log_1m, 0.0)
            lm_hi, lm_lo = _split_hi_lo(log_1m)
            cum = _dot(lm_hi, suo) + _dot(lm_lo, suo)
            carry = carry_ref[p]
            att = jnp.exp(log_beta + cum[:, :t] + carry)
            if diagonal:
                att = jnp.where(valid, att, 0.0)
            part = _dot(att.astype(BF16), jnp.where(head_masks[p], vb, zero_b))
            carry_ref[p] = carry + cum[:, t:]
            out = part if out is None else out + part
        acc_ref[...] += out

    key_block(qi, True)

    def body(i, c):
        key_block(qi - 1 - i, False)
        return c

    lax.fori_loop(0, qi, body, 0)
    o_ref[...] = acc_ref[...].astype(o_ref.dtype)


def _sb_attention(qkv, bsz, seqlen, suo):
    t = SB_BLOCK
    nq = seqlen // t
    n_tiles = D_MODEL // LANES
    return pl.pallas_call(
        _sb_kernel,
        grid=(bsz, n_tiles, nq),
        in_specs=[
            pl.BlockSpec((t, LANES), lambda b, p, i: (b * nq + i, p)),
            pl.BlockSpec((seqlen, LANES), lambda b, p, i: (b, n_tiles + p)),
            pl.BlockSpec((seqlen, LANES), lambda b, p, i: (b, 2 * n_tiles + p)),
            _const_spec((t, 2 * t)),
        ],
        out_specs=pl.BlockSpec((t, LANES), lambda b, p, i: (b * nq + i, p)),
        out_shape=jax.ShapeDtypeStruct((bsz * seqlen, D_MODEL), BF16),
        scratch_shapes=[
            pltpu.VMEM((t, LANES), F32),
            pltpu.VMEM((LANES // SB_HEAD_DIM, t, t), F32),
        ],
        compiler_params=pltpu.CompilerParams(
            dimension_semantics=("parallel", "parallel", "arbitrary"), vmem_limit_bytes=VMEM_LIMIT_BYTES),
        name="sb_attention",
    )(qkv, qkv, qkv, suo)


MLP_FF_CHUNK = 512


def _post_kernel(n_y, x_ref, *refs):
    y_refs = refs[:n_y]
    wo_refs = refs[n_y:2 * n_y]
    nw_ref, w1_ref, w2_ref, o_ref, hid_ref = refs[2 * n_y:]
    x1 = x_ref[...]
    for y_ref, wo_ref in zip(y_refs, wo_refs):
        x1 = x1 + _dot(y_ref[...], wo_ref[...])
    h = _rmsnorm_rows(x1, nw_ref[...]).astype(BF16)
    for c in range(D_FF // MLP_FF_CHUNK):
        sl = slice(c * MLP_FF_CHUNK, (c + 1) * MLP_FF_CHUNK)
        a = jnp.maximum(_dot(h, w1_ref[:, sl]), 0.0)
        hid_ref[:, sl] = (a * a).astype(BF16)
    o_ref[...] = x1 + _dot(hid_ref[...], w2_ref[...])


def _post_block(x2, ys, wos, norm_w, w1, w2, tm=256):
    m = x2.shape[0]
    n_y = len(ys)
    row = lambda width: pl.BlockSpec((tm, width), lambda i: (i, 0))
    return pl.pallas_call(
        functools.partial(_post_kernel, n_y),
        grid=(m // tm,),
        in_specs=[row(D_MODEL)] + [row(y.shape[1]) for y in ys]
        + [_const_spec(w.shape) for w in wos]
        + [_const_spec((1, D_MODEL)), _const_spec((D_MODEL, D_FF)), _const_spec((D_FF, D_MODEL))],
        out_specs=row(D_MODEL),
        out_shape=jax.ShapeDtypeStruct((m, D_MODEL), F32),
        scratch_shapes=[pltpu.VMEM((tm, D_FF), BF16)],
        compiler_params=pltpu.CompilerParams(
            dimension_semantics=("parallel",), vmem_limit_bytes=VMEM_LIMIT_BYTES),
        name="post_block",
    )(x2, *ys, *wos, norm_w, w1, w2)


def _pad_lanes(vec, offset):
    out = jnp.zeros((1, LANES), F32)
    return out.at[0, offset:offset + vec.shape[0]].set(vec.astype(F32))


def _layer0_params(a_norm_w, a_w_in, ssd_conv_w, ssd_conv_b, ssd_dt_bias, ssd_a_log, ssd_d_skip,
                   ssd_norm_w, gdn_conv_w, gdn_a_log, gdn_dt_bias, gdn_norm_w):
    o_z = 0
    o_xbc = o_z + SSD_INNER
    o_dt = o_xbc + SSD_INNER + 2 * SSD_GROUPS * SSD_STATE
    o_qkv = o_dt + SSD_HEADS
    o_gz = o_qkv + 2 * GDN_QK + GDN_VAL
    o_beta = o_gz + GDN_VAL
    o_alpha = o_beta + GDN_HEADS
    w = a_w_in
    w_main = jnp.concatenate(
        [w[:, o_qkv:o_gz], w[:, o_z:o_xbc], w[:, o_gz:o_beta], w[:, o_xbc:o_dt]], axis=1).astype(BF16)
    w_small = jnp.concatenate(
        [w[:, o_dt:o_qkv], w[:, o_beta:o_alpha], w[:, o_alpha:o_alpha + GDN_HEADS],
         jnp.zeros((D_MODEL, LANES - SSD_HEADS - 2 * GDN_HEADS), w.dtype)], axis=1).astype(BF16)
    head_of_lane = jnp.arange(SSD_INNER) // SSD_HEAD_DIM
    e_mat = (jnp.arange(LANES)[:, None] == head_of_lane[None, :]).astype(BF16)
    sel = (jnp.arange(SUBLANES)[:, None] + SM_ALPHA == jnp.arange(LANES)[None, :]).astype(BF16)
    return dict(
        norm_w=a_norm_w.reshape(1, D_MODEL), w_main=w_main, w_small=w_small,
        cwx=ssd_conv_w[:, :SSD_INNER], cbx=ssd_conv_b[:SSD_INNER].reshape(1, -1),
        cwb=ssd_conv_w[:, SSD_INNER:], cbb=ssd_conv_b[SSD_INNER:].reshape(1, -1),
        ssd_dtb=_pad_lanes(ssd_dt_bias, SM_DT), ssd_alog=_pad_lanes(ssd_a_log, SM_DT),
        dsk=jnp.repeat(ssd_d_skip.astype(F32), SSD_HEAD_DIM).reshape(1, -1),
        ssd_nw=ssd_norm_w.reshape(1, -1), e_mat=e_mat,
        gdn_cw=gdn_conv_w, gdn_dtb=_pad_lanes(gdn_dt_bias, SM_ALPHA), gdn_alog=_pad_lanes(gdn_a_log, SM_ALPHA),
        gdn_nw=gdn_norm_w.reshape(1, -1), sel=sel)


def kernel(x, a_norm_w, a_w_in, ssd_conv_w, ssd_conv_b, ssd_dt_bias, ssd_a_log, ssd_d_skip, ssd_norm_w,
           gdn_conv_w, gdn_a_log, gdn_dt_bias, gdn_norm_w, a_w_out, c_norm_w, c_w_qkv, c_q_norm_w,
           c_k_norm_w, c_w_o, mlp_norm_w, mlp_w1, mlp_w2):
    bsz, seqlen, d = x.shape
    x2 = x.reshape(bsz * seqlen, d)

    p = _layer0_params(a_norm_w[0], a_w_in[0], ssd_conv_w[0], ssd_conv_b[0], ssd_dt_bias[0], ssd_a_log[0],
                       ssd_d_skip[0], ssd_norm_w[0], gdn_conv_w[0], gdn_a_log[0], gdn_dt_bias[0], gdn_norm_w[0])
    main, small = _in_proj(x2, p["norm_w"], p["w_main"], p["w_small"])
    y_ssd = _ssd_mixer(main, small, bsz, seqlen, p["cwx"], p["cbx"], p["cwb"], p["cbb"],
                       p["ssd_dtb"], p["ssd_alog"], p["dsk"], p["ssd_nw"], p["e_mat"])
    y_gdn = _gdn_mixer(main, small, bsz, seqlen, p["gdn_cw"], p["gdn_dtb"], p["gdn_alog"], p["gdn_nw"], p["sel"])
    w_out = a_w_out[0].astype(BF16)
    x2 = _post_block(x2, [y_ssd, y_gdn], [w_out[:SSD_INNER], w_out[SSD_INNER:]],
                     mlp_norm_w[0].reshape(1, d), mlp_w1[0].astype(BF16), mlp_w2[0].astype(BF16))

    heads_per_chunk = PROJ_COL_CHUNK // SB_HEAD_DIM
    lane_head = jnp.arange(PROJ_COL_CHUNK) // SB_HEAD_DIM
    gmat = (lane_head[:, None] == lane_head[None, :]).astype(BF16)
    qw = jnp.tile(c_q_norm_w[0].astype(F32) * (SB_HEAD_DIM ** -0.5), heads_per_chunk).reshape(1, -1)
    kw = jnp.tile(c_k_norm_w[0].astype(F32), heads_per_chunk).reshape(1, -1)
    qkv = _qkv_proj(x2, c_norm_w[0].reshape(1, d), c_w_qkv[0].astype(BF16), gmat, qw, kw)
    key = jnp.arange(SB_BLOCK)
    suo = jnp.concatenate([(key[:, None] > key[None, :]).astype(BF16),
                           jnp.ones((SB_BLOCK, SB_BLOCK), BF16)], axis=1)
    o = _sb_attention(qkv, bsz, seqlen, suo)
    x2 = _post_block(x2, [o], [c_w_o[0].astype(BF16)],
                     mlp_norm_w[1].reshape(1, d), mlp_w1[1].astype(BF16), mlp_w2[1].astype(BF16))
    return x2.reshape(bsz, seqlen, d)
```

```python
import functools

import jax
import jax.numpy as jnp
from jax import lax
from jax.experimental import pallas as pl
from jax.experimental.pallas import tpu as pltpu

F32 = jnp.float32
BF16 = jnp.bfloat16

EPS = 1e-6
D_MODEL = 1024
D_FF = 4 * D_MODEL
CONV_WIDTH = 4
SSD_HEADS = 16
SSD_HEAD_DIM = 64
SSD_INNER = SSD_HEADS * SSD_HEAD_DIM
SSD_GROUPS = 2
SSD_STATE = 128
SSD_CHUNK = 128
GDN_HEADS = 8
GDN_DK = 128
GDN_DV = 128
GDN_CHUNK = 64
GDN_QK = GDN_HEADS * GDN_DK
GDN_VAL = GDN_HEADS * GDN_DV
SB_HEADS = 16
SB_HEAD_DIM = D_MODEL // SB_HEADS
SB_BLOCK = 128

LANES = 128
SUBLANES = 8
VMEM_LIMIT_BYTES = 56 * 1024 * 1024

SM_DT = 0
SM_BETA = SSD_HEADS
SM_ALPHA = SSD_HEADS + GDN_HEADS
MAIN_WIDTH = 3 * GDN_QK + SSD_INNER + GDN_VAL + SSD_INNER + 2 * SSD_GROUPS * SSD_STATE
COL_Q, COL_K, COL_V, COL_Z, COL_GZ, COL_XS = 0, 1, 2, 3, 4, 5
COL_BC = (3 * GDN_QK + SSD_INNER + GDN_VAL + SSD_INNER) // (2 * SSD_GROUPS * SSD_STATE)

NEG_BIG = -1e30


def _dot(a, b):
    return jnp.dot(a, b, preferred_element_type=F32)


def _dot_nt(a, b):
    return lax.dot_general(a, b, (((1,), (1,)), ((), ())), preferred_element_type=F32)


def _dot_tn(a, b):
    return lax.dot_general(a, b, (((0,), (0,)), ((), ())), preferred_element_type=F32)


def _split_hi_lo(v):
    hi = v.astype(BF16)
    lo = (v - hi.astype(F32)).astype(BF16)
    return hi, lo


def _softplus(x):
    return jnp.maximum(x, 0.0) + jnp.log1p(jnp.exp(-jnp.abs(x)))


def _sigmoid(x):
    return 1.0 / (1.0 + jnp.exp(-x))


def _silu(x):
    return x * _sigmoid(x)


def _lower_tri(n, dtype):
    row = lax.broadcasted_iota(jnp.int32, (n, n), 0)
    col = lax.broadcasted_iota(jnp.int32, (n, n), 1)
    return (row >= col).astype(dtype)


def _conv_silu(x, tail_ref, w_ref, b_ref):
    t, c = x.shape
    tail = tail_ref[...]
    acc = x * w_ref[CONV_WIDTH - 1:CONV_WIDTH, :]
    if b_ref is not None:
        acc = acc + b_ref[...]
    row8 = lax.broadcasted_iota(jnp.int32, (SUBLANES, c), 0)
    for k in range(1, CONV_WIDTH):
        r = pltpu.roll(x, k, 0)
        tr = pltpu.roll(tail, k, 0)
        first = jnp.where(row8 < k, tr, r[0:SUBLANES])
        shifted = jnp.concatenate([first, r[SUBLANES:]], axis=0)
        acc = acc + shifted * w_ref[CONV_WIDTH - 1 - k:CONV_WIDTH - k, :]
    tail_ref[...] = x[t - SUBLANES:t]
    return _silu(acc)


PROJ_COL_CHUNK = 512


def _rmsnorm_rows(x, w):
    ms = jnp.mean(x * x, axis=-1, keepdims=True)
    return x * lax.rsqrt(ms + EPS) * w


def _in_proj_kernel(x_ref, nw_ref, w_ref, ws_ref, main_ref, small_ref):
    h = _rmsnorm_rows(x_ref[...], nw_ref[...]).astype(BF16)
    small_ref[...] = _dot(h, ws_ref[...])
    for c in range(MAIN_WIDTH // PROJ_COL_CHUNK):
        sl = slice(c * PROJ_COL_CHUNK, (c + 1) * PROJ_COL_CHUNK)
        main_ref[:, sl] = _dot(h, w_ref[:, sl]).astype(main_ref.dtype)


def _const_spec(shape):
    nd = len(shape)
    return pl.BlockSpec(shape, lambda *_: (0,) * nd, pipeline_mode=pl.Buffered(1))


def _in_proj(x2, norm_w, w_main, w_small, tm=256):
    m = x2.shape[0]
    return pl.pallas_call(
        _in_proj_kernel,
        grid=(m // tm,),
        in_specs=[
            pl.BlockSpec((tm, D_MODEL), lambda i: (i, 0)),
            _const_spec((1, D_MODEL)),
            _const_spec((D_MODEL, MAIN_WIDTH)),
            _const_spec((D_MODEL, LANES)),
        ],
        out_specs=[
            pl.BlockSpec((tm, MAIN_WIDTH), lambda i: (i, 0)),
            pl.BlockSpec((tm, LANES), lambda i: (i, 0)),
        ],
        out_shape=[
            jax.ShapeDtypeStruct((m, MAIN_WIDTH), F32),
            jax.ShapeDtypeStruct((m, LANES), F32),
        ],
        compiler_params=pltpu.CompilerParams(
            dimension_semantics=("parallel",), vmem_limit_bytes=VMEM_LIMIT_BYTES),
        name="in_proj",
    )(x2, norm_w, w_main, w_small)


def _qkv_proj_kernel(x_ref, nw_ref, w_ref, gmat_ref, qw_ref, kw_ref, o_ref):
    h = _rmsnorm_rows(x_ref[...], nw_ref[...]).astype(BF16)
    gmat = gmat_ref[...]
    n_chunks = D_MODEL // PROJ_COL_CHUNK
    for part, hw_ref in ((0, qw_ref), (1, kw_ref)):
        for c in range(n_chunks):
            col = part * D_MODEL + c * PROJ_COL_CHUNK
            y = _dot(h, w_ref[:, col:col + PROJ_COL_CHUNK])
            ms = _dot((y * y).astype(BF16), gmat) * (1.0 / SB_HEAD_DIM)
            o_ref[:, col:col + PROJ_COL_CHUNK] = (y * lax.rsqrt(ms + EPS) * hw_ref[...]).astype(o_ref.dtype)
    for c in range(n_chunks):
        col = 2 * D_MODEL + c * PROJ_COL_CHUNK
        o_ref[:, col:col + PROJ_COL_CHUNK] = _dot(h, w_ref[:, col:col + PROJ_COL_CHUNK]).astype(o_ref.dtype)


def _qkv_proj(x2, norm_w, w_qkv, gmat, qw, kw, tm=512):
    m = x2.shape[0]
    return pl.pallas_call(
        _qkv_proj_kernel,
        grid=(m // tm,),
        in_specs=[
            pl.BlockSpec((tm, D_MODEL), lambda i: (i, 0)),
            _const_spec((1, D_MODEL)),
            _const_spec((D_MODEL, 3 * D_MODEL)),
            _const_spec((PROJ_COL_CHUNK, PROJ_COL_CHUNK)),
            _const_spec((1, PROJ_COL_CHUNK)),
            _const_spec((1, PROJ_COL_CHUNK)),
        ],
        out_specs=pl.BlockSpec((tm, 3 * D_MODEL), lambda i: (i, 0)),
        out_shape=jax.ShapeDtypeStruct((m, 3 * D_MODEL), BF16),
        compiler_params=pltpu.CompilerParams(
            dimension_semantics=("parallel",), vmem_limit_bytes=VMEM_LIMIT_BYTES),
        name="qkv_proj",
    )(x2, norm_w, w_qkv, gmat, qw, kw)


def _ssd_kernel(z_ref, xs_ref, bc_ref, sm_ref, cwx_ref, cbx_ref, cwb_ref, cbb_ref,
                dtb_ref, alog_ref, dsk_ref, nw_ref, e_ref,
                y_ref, tailx_ref, tailb_ref, state_ref):
    t = SSD_CHUNK
    n = SSD_STATE
    hpg = SSD_HEADS // SSD_GROUPS
    gw = hpg * SSD_HEAD_DIM

    @pl.when(pl.program_id(1) == 0)
    def _init():
        tailx_ref[...] = jnp.zeros_like(tailx_ref)
        tailb_ref[...] = jnp.zeros_like(tailb_ref)
        state_ref[...] = jnp.zeros_like(state_ref)

    xs = _conv_silu(xs_ref[...], tailx_ref, cwx_ref, cbx_ref)
    bc = _conv_silu(bc_ref[...], tailb_ref, cwb_ref, cbb_ref)

    lane = lax.broadcasted_iota(jnp.int32, (1, LANES), 1)
    head_lane = (lane >= SM_DT) & (lane < SM_DT + SSD_HEADS)
    dt = jnp.where(head_lane, _softplus(sm_ref[...] + dtb_ref[...]), 0.0)
    a = dt * (-jnp.exp(alog_ref[...]))
    tri = _lower_tri(t, BF16)
    a_hi, a_lo = _split_hi_lo(a)
    a_cum = _dot(tri, a_hi) + _dot(tri, a_lo)
    a_cum_t = a_cum.T
    dt_t = dt.T
    total = a_cum[t - 1:t, :]
    e_a = jnp.exp(a_cum)
    w_state = dt * jnp.exp(total - a_cum)

    ex_hi, ex_lo = _split_hi_lo(jnp.concatenate([w_state, e_a], axis=0))
    e_mat = e_ref[...]
    ex = _dot(ex_hi, e_mat) + _dot(ex_lo, e_mat)
    w_state_x = ex[:t]
    e_a_x = ex[t:]
    chunk_decay_x = e_a_x[t - 1:t, :]

    xw = (xs * w_state_x).astype(BF16)
    state_prev = state_ref[...]
    state_prev_b = state_prev.astype(BF16)
    upd, y_off, cb = [], [], []
    for g in range(SSD_GROUPS):
        b_g = bc[:, g * n:(g + 1) * n].astype(BF16)
        c_g = bc[:, (SSD_GROUPS + g) * n:(SSD_GROUPS + g + 1) * n].astype(BF16)
        upd.append(_dot_tn(b_g, xw[:, g * gw:(g + 1) * gw]))
        y_off.append(_dot(c_g, state_prev_b[:, g * gw:(g + 1) * gw]))
        cb.append(_dot_nt(c_g, b_g))
    state_ref[...] = state_prev * chunk_decay_x + jnp.concatenate(upd, axis=1)
    y_off = jnp.concatenate(y_off, axis=1) * e_a_x

    quad = 4
    lane_c = lax.broadcasted_iota(jnp.int32, (1, SSD_INNER), 1)
    head_in_quad = (lane_c // SSD_HEAD_DIM) % quad
    xm = [jnp.where(head_in_quad == j, xs, 0.0).astype(BF16) for j in range(quad)]
    row = lax.broadcasted_iota(jnp.int32, (t, t), 0)
    col = lax.broadcasted_iota(jnp.int32, (t, t), 1)
    causal = row >= col
    qw = quad * SSD_HEAD_DIM
    y_quads = [None] * (SSD_HEADS // quad)
    for h in range(SSD_HEADS):
        g, q, j = h // hpg, h // quad, h % quad
        a_col = jnp.broadcast_to(a_cum[:, h:h + 1], (t, t))
        seg = jnp.where(causal, a_col - a_cum_t[h:h + 1, :], NEG_BIG)
        m = (cb[g] * jnp.exp(seg) * dt_t[h:h + 1, :]).astype(BF16)
        part = _dot(m, xm[j][:, q * qw:(q + 1) * qw])
        y_quads[q] = part if y_quads[q] is None else y_quads[q] + part
    y = jnp.concatenate(y_quads, axis=1) + y_off + xs * dsk_ref[...]

    y = y * _silu(z_ref[...])
    parts = []
    for g in range(SSD_GROUPS):
        yg = y[:, g * gw:(g + 1) * gw]
        ms = jnp.mean(yg * yg, axis=-1, keepdims=True)
        parts.append(yg * lax.rsqrt(ms + EPS))
    y_ref[...] = (jnp.concatenate(parts, axis=1) * nw_ref[...]).astype(y_ref.dtype)


def _ssd_mixer(main, small, bsz, seqlen, cwx, cbx, cwb, cbb, dtb, alog, dsk, nw, e_mat):
    t = SSD_CHUNK
    nc = seqlen // t
    bc_w = 2 * SSD_GROUPS * SSD_STATE
    return pl.pallas_call(
        _ssd_kernel,
        grid=(bsz, nc),
        in_specs=[
            pl.BlockSpec((t, SSD_INNER), lambda b, c: (b * nc + c, COL_Z)),
            pl.BlockSpec((t, SSD_INNER), lambda b, c: (b * nc + c, COL_XS)),
            pl.BlockSpec((t, bc_w), lambda b, c: (b * nc + c, COL_BC)),
            pl.BlockSpec((t, LANES), lambda b, c: (b * nc + c, 0)),
            _const_spec((CONV_WIDTH, SSD_INNER)),
            _const_spec((1, SSD_INNER)),
            _const_spec((CONV_WIDTH, bc_w)),
            _const_spec((1, bc_w)),
            _const_spec((1, LANES)),
            _const_spec((1, LANES)),
            _const_spec((1, SSD_INNER)),
            _const_spec((1, SSD_INNER)),
            _const_spec((LANES, SSD_INNER)),
        ],
        out_specs=pl.BlockSpec((t, SSD_INNER), lambda b, c: (b * nc + c, 0)),
        out_shape=jax.ShapeDtypeStruct((bsz * seqlen, SSD_INNER), BF16),
        scratch_shapes=[
            pltpu.VMEM((SUBLANES, SSD_INNER), F32),
            pltpu.VMEM((SUBLANES, bc_w), F32),
            pltpu.VMEM((SSD_STATE, SSD_INNER), F32),
        ],
        compiler_params=pltpu.CompilerParams(
            dimension_semantics=("parallel", "arbitrary"), vmem_limit_bytes=VMEM_LIMIT_BYTES),
        name="ssd_mixer",
    )(main, main, main, small, cwx, cbx, cwb, cbb, dtb, alog, dsk, nw, e_mat)


def _l2norm_rows(x):
    return x * lax.rsqrt(jnp.sum(x * x, axis=-1, keepdims=True) + EPS)


GDN_STEP_CHUNKS = 4
GDN_PAIR = 2
GDN_QUAD = 4


def _pair_block_diag(m):
    lane = lax.broadcasted_iota(jnp.int32, m.shape, 1)
    zero = jnp.zeros((), m.dtype)
    return jnp.concatenate([jnp.where(lane < LANES, m, zero), jnp.where(lane >= LANES, m, zero)], axis=0)


def _packed_rows(rows8):
    n_pairs = GDN_HEADS // 2
    head = lax.broadcasted_iota(jnp.int32, rows8.shape, 0)
    m = jnp.where((head & 1) == 1, pltpu.roll(rows8, GDN_CHUNK, 1), rows8)
    m4 = jnp.concatenate([m] * n_pairs, axis=1)
    head4 = lax.broadcasted_iota(jnp.int32, m4.shape, 0)
    pair_of_lane = lax.broadcasted_iota(jnp.int32, m4.shape, 1) // LANES
    y = jnp.where((head4 >> 1) == pair_of_lane, m4, 0.0)
    for shift in (4, 2, 1):
        y = y + pltpu.roll(y, shift, 0)
    return jnp.concatenate([y] * (GDN_CHUNK // SUBLANES), axis=0)


def _gdn_kernel(q_ref, k_ref, v_ref, gz_ref, sm_ref, cwq_ref, cwk_ref, cwv_ref,
                dtb_ref, alog_ref, nw_ref, e64_ref,
                o_ref, tq_ref, tk_ref, tv_ref, s_ref):
    t = GDN_CHUNK

    @pl.when(pl.program_id(1) == 0)
    def _init():
        tq_ref[...] = jnp.zeros_like(tq_ref)
        tk_ref[...] = jnp.zeros_like(tk_ref)
        tv_ref[...] = jnp.zeros_like(tv_ref)
        s_ref[...] = jnp.zeros_like(s_ref)

    q = _conv_silu(q_ref[...], tq_ref, cwq_ref, None)
    k = _conv_silu(k_ref[...], tk_ref, cwk_ref, None)
    v_b = _conv_silu(v_ref[...], tv_ref, cwv_ref, None).astype(BF16)
    sm = sm_ref[...]
    gz = gz_ref[...]
    nw = nw_ref[...]
    e64 = e64_ref[...]
    nc = GDN_STEP_CHUNKS
    chunks = range(nc)
    pairs = range(GDN_HEADS // GDN_PAIR)
    quads = range(GDN_HEADS // GDN_QUAD)
    rows = [slice(c * t, (c + 1) * t) for c in chunks]
    pw = GDN_HEADS * t
    pair_w = GDN_PAIR * GDN_DK
    pair_pw = GDN_PAIR * t
    psl = [slice(p * pair_w, (p + 1) * pair_w) for p in pairs]
    ppl = [slice(p * pair_pw, (p + 1) * pair_pw) for p in pairs]

    qn, kn = [], []
    for h in range(GDN_HEADS):
        sl = slice(h * GDN_DK, (h + 1) * GDN_DK)
        qn.append(_l2norm_rows(q[:, sl]) * (GDN_DK ** -0.5))
        kn.append(_l2norm_rows(k[:, sl]))
    q_b = jnp.concatenate(qn, axis=1).astype(BF16)
    k_b = jnp.concatenate(kn, axis=1).astype(BF16)

    lane = lax.broadcasted_iota(jnp.int32, (1, LANES), 1)
    alpha_lane = (lane >= SM_ALPHA) & (lane < SM_ALPHA + GDN_HEADS)
    beta_lane = (lane >= SM_BETA) & (lane < SM_BETA + GDN_HEADS)
    beta = jnp.where(beta_lane, _sigmoid(sm), 0.0)
    g = jnp.where(alpha_lane, -jnp.exp(alog_ref[...]) * _softplus(sm + dtb_ref[...]), 0.0)
    row_s = lax.broadcasted_iota(jnp.int32, (nc * t, nc * t), 0)
    col_s = lax.broadcasted_iota(jnp.int32, (nc * t, nc * t), 1)
    tri = ((row_s >= col_s) & ((row_s // t) == (col_s // t))).astype(BF16)
    g_hi, g_lo = _split_hi_lo(g)
    gc_all = _dot(tri, g_hi) + _dot(tri, g_lo)
    e_gc_all = jnp.exp(gc_all)
    gc = [gc_all[r] for r in rows]
    e_gc = [e_gc_all[r] for r in rows]
    e_k = [jnp.exp(gc[c][t - 1:t, :] - gc[c]) for c in chunks]

    pad = jnp.zeros((LANES - t, LANES), F32)
    gc_rows8 = [jnp.concatenate([gc[c], pad], axis=0).T[SM_ALPHA:SM_ALPHA + GDN_HEADS, :] for c in chunks]
    beta_rows8 = [jnp.concatenate([beta[rows[c]], pad], axis=0).T[SM_BETA:SM_BETA + GDN_HEADS, :] for c in chunks]
    gc_rp = [_packed_rows(gc_rows8[c]) for c in chunks]
    beta_rp = [_packed_rows(beta_rows8[c]) for c in chunks]
    be_rp = [_packed_rows(beta_rows8[c] * jnp.exp(gc_rows8[c])) for c in chunks]
    gc_hi, gc_lo = _split_hi_lo(gc_all)
    ex = _dot(jnp.concatenate([gc_hi, gc_lo, beta.astype(BF16)], axis=0), e64)
    gc_cp_all = ex[:nc * t] + ex[nc * t:2 * nc * t]
    beta_cp_all = ex[2 * nc * t:]

    rowp = lax.broadcasted_iota(jnp.int32, (t, pw), 0)
    colp = lax.broadcasted_iota(jnp.int32, (t, pw), 1) & (t - 1)
    causal_p = colp <= rowp
    strict_p = colp < rowp
    decay_p = [jnp.exp(jnp.where(causal_p, gc_cp_all[rows[c]] - gc_rp[c], NEG_BIG)) for c in chunks]

    kq = [[_dot_nt(jnp.concatenate([k_b[rows[c], psl[p]], q_b[rows[c], psl[p]]], axis=0),
                   _pair_block_diag(k_b[rows[c], psl[p]])) for p in pairs] for c in chunks]
    amat = [jnp.where(strict_p, jnp.concatenate([kq[c][p][:t] for p in pairs], axis=1)
                      * beta_cp_all[rows[c]] * decay_p[c], 0.0) for c in chunks]
    qk_b = [jnp.where(causal_p, jnp.concatenate([kq[c][p][t:] for p in pairs], axis=1) * decay_p[c], 0.0).astype(BF16)
            for c in chunks]

    qd = GDN_QUAD * t
    blk_r = lax.broadcasted_iota(jnp.int32, (qd, qd), 0) // t
    blk_c = lax.broadcasted_iota(jnp.int32, (qd, qd), 1) // t
    same_block = blk_r == blk_c
    eye_q = ((lax.broadcasted_iota(jnp.int32, (t, qd), 1) & (t - 1))
             == lax.broadcasted_iota(jnp.int32, (t, qd), 0)).astype(F32)

    def block_diag4(m):
        return jnp.where(same_block, jnp.concatenate([m] * GDN_QUAD, axis=0), 0.0).astype(BF16)

    units = [(c, d) for c in chunks for d in quads]
    a_q = {u: amat[u[0]][:, u[1] * qd:(u[1] + 1) * qd] for u in units}
    x_q = {u: eye_q - a_q[u] for u in units}
    p_q = {u: _dot(a_q[u].astype(BF16), block_diag4(a_q[u])) for u in units}
    n_sq = (t - 1).bit_length() - 1
    for it in range(n_sq):
        last = it + 1 == n_sq
        xp = {u: _dot((x_q[u] if last else jnp.concatenate([x_q[u], p_q[u]], axis=0)).astype(BF16),
                      block_diag4(p_q[u])) for u in units}
        x_q = {u: x_q[u] + xp[u][:t] for u in units}
        if not last:
            p_q = {u: xp[u][t:] for u in units}
    t_inv = [jnp.concatenate([x_q[(c, d)] for d in quads], axis=1) for c in chunks]
    tb = [(t_inv[c] * beta_rp[c]).astype(BF16) for c in chunks]
    tbe = [(t_inv[c] * be_rp[c]).astype(BF16) for c in chunks]

    u_base = [jnp.concatenate([_dot(tb[c][:, ppl[p]], _pair_block_diag(v_b[rows[c], psl[p]])) for p in pairs], axis=1)
              for c in chunks]
    w_b = [[_dot(tbe[c][:, ppl[p]], _pair_block_diag(k_b[rows[c], psl[p]])).astype(BF16) for p in pairs]
           for c in chunks]

    def expand_cols(x, base):
        return jnp.concatenate(
            [jnp.broadcast_to(x[:, base + h:base + h + 1], (t, GDN_DV)) for h in range(GDN_HEADS)], axis=1)

    e_gc_x = [expand_cols(e_gc[c], SM_ALPHA) for c in chunks]
    e_k_x = [expand_cols(e_k[c], SM_ALPHA) for c in chunks]

    s_all = s_ref[...]
    o_chunks = []
    for c in chunks:
        s_b = s_all.astype(BF16)
        r = [_dot(jnp.concatenate([w_b[c][p], q_b[rows[c], psl[p]]], axis=0), _pair_block_diag(s_b[:, psl[p]]))
             for p in pairs]
        u = u_base[c] - jnp.concatenate([r[p][:t] for p in pairs], axis=1)
        u_b = u.astype(BF16)
        u_dec_b = (u * e_k_x[c]).astype(BF16)
        o2 = [_dot(qk_b[c][:, ppl[p]], _pair_block_diag(u_b[:, psl[p]])) for p in pairs]
        full = [_dot_tn(k_b[rows[c], psl[p]], u_dec_b[:, psl[p]]) for p in pairs]
        upd = jnp.concatenate([blk for p in pairs for blk in (full[p][:GDN_DK, :GDN_DV], full[p][GDN_DK:, GDN_DV:])],
                              axis=1)
        s_all = s_all * e_gc_x[c][t - 1:t, :] + upd
        o_chunks.append(jnp.concatenate([r[p][t:] for p in pairs], axis=1) * e_gc_x[c] + jnp.concatenate(o2, axis=1))
    s_ref[...] = s_all

    o = jnp.concatenate(o_chunks, axis=0)
    outs = []
    for h in range(GDN_HEADS):
        sl = slice(h * GDN_DV, (h + 1) * GDN_DV)
        oh = o[:, sl]
        oh = oh * lax.rsqrt(jnp.mean(oh * oh, axis=-1, keepdims=True) + EPS) * nw
        outs.append((oh * _silu(gz[:, sl])).astype(o_ref.dtype))
    o_ref[...] = jnp.concatenate(outs, axis=1)


def _gdn_mixer(main, small, bsz, seqlen, cw, dtb, alog, nw, e64):
    t = GDN_STEP_CHUNKS * GDN_CHUNK
    nc = seqlen // t
    row_spec = lambda colblk: pl.BlockSpec((t, GDN_QK), lambda b, c: (b * nc + c, colblk))
    cw_spec = lambda colblk: pl.BlockSpec((CONV_WIDTH, GDN_QK), lambda b, c: (0, colblk),
                                          pipeline_mode=pl.Buffered(1))
    return pl.pallas_call(
        _gdn_kernel,
        grid=(bsz, nc),
        in_specs=[
            row_spec(COL_Q), row_spec(COL_K), row_spec(COL_V), row_spec(COL_GZ),
            pl.BlockSpec((t, LANES), lambda b, c: (b * nc + c, 0)),
            cw_spec(0), cw_spec(1), cw_spec(2),
            _const_spec((1, LANES)),
            _const_spec((1, LANES)),
            _const_spec((1, GDN_DV)),
            _const_spec((LANES, GDN_HEADS * GDN_CHUNK)),
        ],
        out_specs=pl.BlockSpec((t, GDN_VAL), lambda b, c: (b * nc + c, 0)),
        out_shape=jax.ShapeDtypeStruct((bsz * seqlen, GDN_VAL), BF16),
        scratch_shapes=[
            pltpu.VMEM((SUBLANES, GDN_QK), F32),
            pltpu.VMEM((SUBLANES, GDN_QK), F32),
            pltpu.VMEM((SUBLANES, GDN_VAL), F32),
            pltpu.VMEM((GDN_DK, GDN_HEADS * GDN_DV), F32),
        ],
        compiler_params=pltpu.CompilerParams(
            dimension_semantics=("parallel", "arbitrary"), vmem_limit_bytes=VMEM_LIMIT_BYTES),
        name="gdn_mixer",
    )(main, main, main, main, small, cw, cw, cw, dtb, alog, nw, e64)


SB_TILE = 256
SB_LANE_TILES = 4
SB_HEADS_PER_LANE_TILE = LANES // SB_HEAD_DIM


def _sb_kernel(q_ref, k_ref, v_ref, su_ref, o_ref, acc_ref, carry_ref):
    t = SB_TILE
    qi = pl.program_id(2)
    lane = lax.broadcasted_iota(jnp.int32, (1, LANES), 1)
    head_masks = [(lane // SB_HEAD_DIM) == p for p in range(SB_HEADS_PER_LANE_TILE)]
    zero_b = jnp.zeros((), BF16)
    q = q_ref[...]
    qm = [[jnp.where(hm, q[:, g * LANES:(g + 1) * LANES], zero_b) for hm in head_masks]
          for g in range(SB_LANE_TILES)]
    row = lax.broadcasted_iota(jnp.int32, (t, t), 0)
    col = lax.broadcasted_iota(jnp.int32, (t, t), 1)
    valid = col < row
    su = su_ref[...]

    acc_ref[...] = jnp.zeros_like(acc_ref)
    carry_ref[...] = jnp.zeros_like(carry_ref)

    def key_tile(j, diagonal):
        start = pl.multiple_of(j * t, t)
        k_all = k_ref[pl.ds(start, t), :]
        v_all = v_ref[pl.ds(start, t), :]
        carry_all = carry_ref[...]
        outs, carries = [], []
        for g in range(SB_LANE_TILES):
            kb = k_all[:, g * LANES:(g + 1) * LANES]
            vb = v_all[:, g * LANES:(g + 1) * LANES]
            out = None
            for p in range(SB_HEADS_PER_LANE_TILE):
                hh = g * SB_HEADS_PER_LANE_TILE + p
                s = _dot_nt(qm[g][p], kb)
                soft = jnp.log(1.0 + jnp.exp(-jnp.abs(s)))
                log_beta = jnp.minimum(s, 0.0) - soft
                log_1m = log_beta - s
                if diagonal:
                    log_1m = jnp.where(valid, log_1m, 0.0)
                lm_hi, lm_lo = _split_hi_lo(log_1m)
                suffix = _dot(lm_hi, su) + _dot(lm_lo, su)
                carry = carry_all[:, hh * LANES:(hh + 1) * LANES]
                att = jnp.exp(log_beta + suffix + jnp.concatenate([carry] * (t // LANES), axis=1))
                if diagonal:
                    att = jnp.where(valid, att, 0.0)
                part = _dot(att.astype(BF16), jnp.where(head_masks[p], vb, zero_b))
                carries.append(carry + jnp.sum(log_1m, axis=-1, keepdims=True))
                out = part if out is None else out + part
            outs.append(out)
        acc_ref[...] += jnp.concatenate(outs, axis=1)
        carry_ref[...] = jnp.concatenate(carries, axis=1)

    key_tile(qi, True)

    def body(i, c):
        key_tile(qi - 1 - i, False)
        return c

    lax.fori_loop(0, qi, body, 0)
    o_ref[...] = acc_ref[...].astype(o_ref.dtype)


def _sb_attention(qkv, bsz, seqlen, su):
    t = SB_TILE
    nq = seqlen // t
    w = SB_LANE_TILES * LANES
    n_groups = D_MODEL // w
    return pl.pallas_call(
        _sb_kernel,
        grid=(bsz, n_groups, nq),
        in_specs=[
            pl.BlockSpec((t, w), lambda b, p, i: (b * nq + i, p)),
            pl.BlockSpec((seqlen, w), lambda b, p, i: (b, n_groups + p)),
            pl.BlockSpec((seqlen, w), lambda b, p, i: (b, 2 * n_groups + p)),
            _const_spec((t, t)),
        ],
        out_specs=pl.BlockSpec((t, w), lambda b, p, i: (b * nq + i, p)),
        out_shape=jax.ShapeDtypeStruct((bsz * seqlen, D_MODEL), BF16),
        scratch_shapes=[
            pltpu.VMEM((t, w), F32),
            pltpu.VMEM((t, SB_LANE_TILES * SB_HEADS_PER_LANE_TILE * LANES), F32),
        ],
        compiler_params=pltpu.CompilerParams(
            dimension_semantics=("parallel", "parallel", "arbitrary"), vmem_limit_bytes=VMEM_LIMIT_BYTES),
        name="sb_attention",
    )(qkv, qkv, qkv, su)


MLP_FF_CHUNK = 512


def _post_kernel(n_y, x_ref, *refs):
    y_refs = refs[:n_y]
    wo_refs = refs[n_y:2 * n_y]
    nw_ref, w1_ref, w2_ref, o_ref, hid_ref = refs[2 * n_y:]
    x1 = x_ref[...]
    for y_ref, wo_ref in zip(y_refs, wo_refs):
        x1 = x1 + _dot(y_ref[...], wo_ref[...])
    h = _rmsnorm_rows(x1, nw_ref[...]).astype(BF16)
    for c in range(D_FF // MLP_FF_CHUNK):
        sl = slice(c * MLP_FF_CHUNK, (c + 1) * MLP_FF_CHUNK)
        a = jnp.maximum(_dot(h, w1_ref[:, sl]), 0.0)
        hid_ref[:, sl] = (a * a).astype(BF16)
    o_ref[...] = x1 + _dot(hid_ref[...], w2_ref[...])


def _post_block(x2, ys, wos, norm_w, w1, w2, tm=256):
    m = x2.shape[0]
    n_y = len(ys)
    row = lambda width: pl.BlockSpec((tm, width), lambda i: (i, 0))
    return pl.pallas_call(
        functools.partial(_post_kernel, n_y),
        grid=(m // tm,),
        in_specs=[row(D_MODEL)] + [row(y.shape[1]) for y in ys]
        + [_const_spec(w.shape) for w in wos]
        + [_const_spec((1, D_MODEL)), _const_spec((D_MODEL, D_FF)), _const_spec((D_FF, D_MODEL))],
        out_specs=row(D_MODEL),
        out_shape=jax.ShapeDtypeStruct((m, D_MODEL), F32),
        scratch_shapes=[pltpu.VMEM((tm, D_FF), BF16)],
        compiler_params=pltpu.CompilerParams(
            dimension_semantics=("parallel",), vmem_limit_bytes=VMEM_LIMIT_BYTES),
        name="post_block",
    )(x2, *ys, *wos, norm_w, w1, w2)


def _pad_lanes(vec, offset):
    out = jnp.zeros((1, LANES), F32)
    return out.at[0, offset:offset + vec.shape[0]].set(vec.astype(F32))


def _layer0_params(a_norm_w, a_w_in, ssd_conv_w, ssd_conv_b, ssd_dt_bias, ssd_a_log, ssd_d_skip,
                   ssd_norm_w, gdn_conv_w, gdn_a_log, gdn_dt_bias, gdn_norm_w):
    o_z = 0
    o_xbc = o_z + SSD_INNER
    o_dt = o_xbc + SSD_INNER + 2 * SSD_GROUPS * SSD_STATE
    o_qkv = o_dt + SSD_HEADS
    o_gz = o_qkv + 2 * GDN_QK + GDN_VAL
    o_beta = o_gz + GDN_VAL
    o_alpha = o_beta + GDN_HEADS
    w = a_w_in
    w_main = jnp.concatenate(
        [w[:, o_qkv:o_gz], w[:, o_z:o_xbc], w[:, o_gz:o_beta], w[:, o_xbc:o_dt]], axis=1).astype(BF16)
    w_small = jnp.concatenate(
        [w[:, o_dt:o_qkv], w[:, o_beta:o_alpha], w[:, o_alpha:o_alpha + GDN_HEADS],
         jnp.zeros((D_MODEL, LANES - SSD_HEADS - 2 * GDN_HEADS), w.dtype)], axis=1).astype(BF16)
    head_of_lane = jnp.arange(SSD_INNER) // SSD_HEAD_DIM
    e_mat = (jnp.arange(LANES)[:, None] == head_of_lane[None, :]).astype(BF16)
    head_of_packed = jnp.arange(GDN_HEADS * GDN_CHUNK) // GDN_CHUNK
    src = jnp.arange(LANES)[:, None]
    e64 = ((src == head_of_packed[None, :] + SM_BETA) | (src == head_of_packed[None, :] + SM_ALPHA)).astype(BF16)
    return dict(
        norm_w=a_norm_w.reshape(1, D_MODEL), w_main=w_main, w_small=w_small,
        cwx=ssd_conv_w[:, :SSD_INNER], cbx=ssd_conv_b[:SSD_INNER].reshape(1, -1),
        cwb=ssd_conv_w[:, SSD_INNER:], cbb=ssd_conv_b[SSD_INNER:].reshape(1, -1),
        ssd_dtb=_pad_lanes(ssd_dt_bias, SM_DT), ssd_alog=_pad_lanes(ssd_a_log, SM_DT),
        dsk=jnp.repeat(ssd_d_skip.astype(F32), SSD_HEAD_DIM).reshape(1, -1),
        ssd_nw=ssd_norm_w.reshape(1, -1), e_mat=e_mat,
        gdn_cw=gdn_conv_w, gdn_dtb=_pad_lanes(gdn_dt_bias, SM_ALPHA), gdn_alog=_pad_lanes(gdn_a_log, SM_ALPHA),
        gdn_nw=gdn_norm_w.reshape(1, -1), e64=e64)


def kernel(x, a_norm_w, a_w_in, ssd_conv_w, ssd_conv_b, ssd_dt_bias, ssd_a_log, ssd_d_skip, ssd_norm_w,
           gdn_conv_w, gdn_a_log, gdn_dt_bias, gdn_norm_w, a_w_out, c_norm_w, c_w_qkv, c_q_norm_w,
           c_k_norm_w, c_w_o, mlp_norm_w, mlp_w1, mlp_w2):
    bsz, seqlen, d = x.shape
    x2 = x.reshape(bsz * seqlen, d)

    p = _layer0_params(a_norm_w[0], a_w_in[0], ssd_conv_w[0], ssd_conv_b[0], ssd_dt_bias[0], ssd_a_log[0],
                       ssd_d_skip[0], ssd_norm_w[0], gdn_conv_w[0], gdn_a_log[0], gdn_dt_bias[0], gdn_norm_w[0])
    main, small = _in_proj(x2, p["norm_w"], p["w_main"], p["w_small"])
    y_ssd = _ssd_mixer(main, small, bsz, seqlen, p["cwx"], p["cbx"], p["cwb"], p["cbb"],
                       p["ssd_dtb"], p["ssd_alog"], p["dsk"], p["ssd_nw"], p["e_mat"])
    y_gdn = _gdn_mixer(main, small, bsz, seqlen, p["gdn_cw"], p["gdn_dtb"], p["gdn_alog"], p["gdn_nw"], p["e64"])
    w_out = a_w_out[0].astype(BF16)
    x2 = _post_block(x2, [y_ssd, y_gdn], [w_out[:SSD_INNER], w_out[SSD_INNER:]],
                     mlp_norm_w[0].reshape(1, d), mlp_w1[0].astype(BF16), mlp_w2[0].astype(BF16))

    heads_per_chunk = PROJ_COL_CHUNK // SB_HEAD_DIM
    lane_head = jnp.arange(PROJ_COL_CHUNK) // SB_HEAD_DIM
    gmat = (lane_head[:, None] == lane_head[None, :]).astype(BF16)
    qw = jnp.tile(c_q_norm_w[0].astype(F32) * (SB_HEAD_DIM ** -0.5), heads_per_chunk).reshape(1, -1)
    kw = jnp.tile(c_k_norm_w[0].astype(F32), heads_per_chunk).reshape(1, -1)
    qkv = _qkv_proj(x2, c_norm_w[0].reshape(1, d), c_w_qkv[0].astype(BF16), gmat, qw, kw)
    key = jnp.arange(SB_TILE)
    su = (key[:, None] > key[None, :]).astype(BF16)
    o = _sb_attention(qkv, bsz, seqlen, su)
    x2 = _post_block(x2, [o], [c_w_o[0].astype(BF16)],
                     mlp_norm_w[1].reshape(1, d), mlp_w1[1].astype(BF16), mlp_w2[1].astype(BF16))
    return x2.reshape(bsz, seqlen, d)
```

```python
import functools

import jax
import jax.numpy as jnp
from jax import lax
from jax.experimental import pallas as pl
from jax.experimental.pallas import tpu as pltpu

F32 = jnp.float32
BF16 = jnp.bfloat16

EPS = 1e-6
D_MODEL = 1024
D_FF = 4 * D_MODEL
CONV_WIDTH = 4
SSD_HEADS = 16
SSD_HEAD_DIM = 64
SSD_INNER = SSD_HEADS * SSD_HEAD_DIM
SSD_GROUPS = 2
SSD_STATE = 128
SSD_CHUNK = 128
GDN_HEADS = 8
GDN_DK = 128
GDN_DV = 128
GDN_CHUNK = 64
GDN_QK = GDN_HEADS * GDN_DK
GDN_VAL = GDN_HEADS * GDN_DV
SB_HEADS = 16
SB_HEAD_DIM = D_MODEL // SB_HEADS
SB_BLOCK = 128

LANES = 128
SUBLANES = 8
VMEM_LIMIT_BYTES = 56 * 1024 * 1024

SM_DT = 0
SM_BETA = SSD_HEADS
SM_ALPHA = SSD_HEADS + GDN_HEADS
MAIN_WIDTH = 3 * GDN_QK + SSD_INNER + GDN_VAL + SSD_INNER + 2 * SSD_GROUPS * SSD_STATE
COL_Q, COL_K, COL_V, COL_Z, COL_GZ, COL_XS = 0, 1, 2, 3, 4, 5
COL_BC = (3 * GDN_QK + SSD_INNER + GDN_VAL + SSD_INNER) // (2 * SSD_GROUPS * SSD_STATE)

NEG_BIG = -1e30
LOG2_E = 1.4426950408889634


def _dot(a, b):
    return jnp.dot(a, b, preferred_element_type=F32)


def _dot_nt(a, b):
    return lax.dot_general(a, b, (((1,), (1,)), ((), ())), preferred_element_type=F32)


def _dot_tn(a, b):
    return lax.dot_general(a, b, (((0,), (0,)), ((), ())), preferred_element_type=F32)


def _split_hi_lo(v):
    hi = v.astype(BF16)
    lo = (v - hi.astype(F32)).astype(BF16)
    return hi, lo


def _softplus(x):
    return jnp.maximum(x, 0.0) + jnp.log1p(jnp.exp(-jnp.abs(x)))


def _sigmoid(x):
    return 1.0 / (1.0 + jnp.exp(-x))


def _silu(x):
    return x * _sigmoid(x)


def _lower_tri(n, dtype):
    row = lax.broadcasted_iota(jnp.int32, (n, n), 0)
    col = lax.broadcasted_iota(jnp.int32, (n, n), 1)
    return (row >= col).astype(dtype)


def _conv_silu(x, tail, w, b):
    t, c = x.shape
    acc = x * w[CONV_WIDTH - 1:CONV_WIDTH, :] + b
    row8 = lax.broadcasted_iota(jnp.int32, (SUBLANES, c), 0)
    for k in range(1, CONV_WIDTH):
        r = pltpu.roll(x, k, 0)
        tr = pltpu.roll(tail, k, 0)
        first = jnp.where(row8 < k, tr, r[0:SUBLANES])
        shifted = jnp.concatenate([first, r[SUBLANES:]], axis=0)
        acc = acc + shifted * w[CONV_WIDTH - 1 - k:CONV_WIDTH - k, :]
    return _silu(acc), x[t - SUBLANES:t]


PROJ_COL_CHUNK = 512


def _rmsnorm_rows(x, w):
    ms = jnp.mean(x * x, axis=-1, keepdims=True)
    return x * lax.rsqrt(ms + EPS) * w


GATE_COLS = (COL_Z * GDN_QK, (COL_GZ + 1) * GDN_QK)
RAW_COLS = (COL_XS * SSD_INNER, (COL_XS + 1) * SSD_INNER)


def _in_proj_kernel(tiles_per_seq, x_ref, nw_ref, w_ref, ws_ref, cw_ref, cb_ref, main_ref, small_ref,
                    tail_ref, win_ref):
    tm = x_ref.shape[0]

    @pl.when(pl.program_id(0) % tiles_per_seq == 0)
    def _init():
        tail_ref[...] = jnp.zeros_like(tail_ref)

    h = _rmsnorm_rows(x_ref[...], nw_ref[...]).astype(BF16)
    small_ref[...] = _dot(h, ws_ref[...])
    for c in range(MAIN_WIDTH // PROJ_COL_CHUNK):
        sl = slice(c * PROJ_COL_CHUNK, (c + 1) * PROJ_COL_CHUNK)
        y = _dot(h, w_ref[:, sl])
        if GATE_COLS[0] <= sl.start < GATE_COLS[1]:
            main_ref[:, sl] = _silu(y)
            continue
        if RAW_COLS[0] <= sl.start < RAW_COLS[1]:
            main_ref[:, sl] = y
            continue
        rep = lambda a: pltpu.repeat(a, tm // SUBLANES, 0)
        w0, w1, w2, w3 = (rep(cw_ref[k * SUBLANES:(k + 1) * SUBLANES, sl]) for k in range(CONV_WIDTH))
        bias = rep(cb_ref[:, sl])
        win_y = win_ref.at[0, c % 2]
        win_u = win_ref.at[1, c % 2]
        win_y[0:SUBLANES, :] = tail_ref[0, :, sl]
        win_y[SUBLANES:, :] = y
        tail_ref[0, :, sl] = y[tm - SUBLANES:tm]
        y1 = win_y[pl.ds(SUBLANES - 1, tm), :]
        u = y * w1 + y1 * w0
        win_u[0:SUBLANES, :] = tail_ref[1, :, sl]
        win_u[SUBLANES:, :] = u
        tail_ref[1, :, sl] = u[tm - SUBLANES:tm]
        acc = y * w3 + y1 * w2 + win_u[pl.ds(SUBLANES - 2, tm), :] + bias
        main_ref[:, sl] = _silu(acc)


def _const_spec(shape):
    nd = len(shape)
    return pl.BlockSpec(shape, lambda *_: (0,) * nd, pipeline_mode=pl.Buffered(1))


def _in_proj(x2, seqlen, norm_w, w_main, w_small, conv_w, conv_b, tm=256):
    m = x2.shape[0]
    return pl.pallas_call(
        functools.partial(_in_proj_kernel, seqlen // tm),
        grid=(m // tm,),
        in_specs=[
            pl.BlockSpec((tm, D_MODEL), lambda i: (i, 0)),
            _const_spec((1, D_MODEL)),
            _const_spec((D_MODEL, MAIN_WIDTH)),
            _const_spec((D_MODEL, LANES)),
            _const_spec((CONV_WIDTH * SUBLANES, MAIN_WIDTH)),
            _const_spec((SUBLANES, MAIN_WIDTH)),
        ],
        out_specs=[
            pl.BlockSpec((tm, MAIN_WIDTH), lambda i: (i, 0)),
            pl.BlockSpec((tm, LANES), lambda i: (i, 0)),
        ],
        out_shape=[
            jax.ShapeDtypeStruct((m, MAIN_WIDTH), F32),
            jax.ShapeDtypeStruct((m, LANES), F32),
        ],
        scratch_shapes=[pltpu.VMEM((2, SUBLANES, MAIN_WIDTH), F32),
                        pltpu.VMEM((2, 2, SUBLANES + tm, PROJ_COL_CHUNK), F32)],
        compiler_params=pltpu.CompilerParams(
            dimension_semantics=("arbitrary",), vmem_limit_bytes=VMEM_LIMIT_BYTES),
        name="in_proj",
    )(x2, norm_w, w_main, w_small, conv_w, conv_b)


def _qkv_proj_kernel(x_ref, nw_ref, w_ref, gmat_ref, qw_ref, kw_ref, o_ref):
    h = _rmsnorm_rows(x_ref[...], nw_ref[...]).astype(BF16)
    gmat = gmat_ref[...]
    n_chunks = D_MODEL // PROJ_COL_CHUNK
    for part, hw_ref in ((0, qw_ref), (1, kw_ref)):
        for c in range(n_chunks):
            col = part * D_MODEL + c * PROJ_COL_CHUNK
            y = _dot(h, w_ref[:, col:col + PROJ_COL_CHUNK])
            ms = _dot((y * y).astype(BF16), gmat) * (1.0 / SB_HEAD_DIM)
            o_ref[:, col:col + PROJ_COL_CHUNK] = (y * lax.rsqrt(ms + EPS) * hw_ref[...]).astype(o_ref.dtype)
    for c in range(n_chunks):
        col = 2 * D_MODEL + c * PROJ_COL_CHUNK
        o_ref[:, col:col + PROJ_COL_CHUNK] = _dot(h, w_ref[:, col:col + PROJ_COL_CHUNK]).astype(o_ref.dtype)


def _qkv_proj(x2, norm_w, w_qkv, gmat, qw, kw, tm=512):
    m = x2.shape[0]
    return pl.pallas_call(
        _qkv_proj_kernel,
        grid=(m // tm,),
        in_specs=[
            pl.BlockSpec((tm, D_MODEL), lambda i: (i, 0)),
            _const_spec((1, D_MODEL)),
            _const_spec((D_MODEL, 3 * D_MODEL)),
            _const_spec((PROJ_COL_CHUNK, PROJ_COL_CHUNK)),
            _const_spec((1, PROJ_COL_CHUNK)),
            _const_spec((1, PROJ_COL_CHUNK)),
        ],
        out_specs=pl.BlockSpec((tm, 3 * D_MODEL), lambda i: (i, 0)),
        out_shape=jax.ShapeDtypeStruct((m, 3 * D_MODEL), BF16),
        compiler_params=pltpu.CompilerParams(
            dimension_semantics=("parallel",), vmem_limit_bytes=VMEM_LIMIT_BYTES),
        name="qkv_proj",
    )(x2, norm_w, w_qkv, gmat, qw, kw)


def _ssd_kernel(gate_ref, xs_ref, bc_ref, sm_ref, cwx_ref, cbx_ref, dtb_ref, alog_ref, dsk_ref, nw_ref, e_ref,
                y_ref, tailx_ref, state_ref):
    t = SSD_CHUNK
    n = SSD_STATE
    hpg = SSD_HEADS // SSD_GROUPS
    gw = hpg * SSD_HEAD_DIM

    @pl.when(pl.program_id(1) == 0)
    def _init():
        tailx_ref[...] = jnp.zeros_like(tailx_ref)
        state_ref[...] = jnp.zeros_like(state_ref)

    xs, tailx_ref[...] = _conv_silu(xs_ref[...], tailx_ref[...], cwx_ref[...], cbx_ref[...])
    bc = bc_ref[...]

    lane = lax.broadcasted_iota(jnp.int32, (1, LANES), 1)
    head_lane = (lane >= SM_DT) & (lane < SM_DT + SSD_HEADS)
    dt = jnp.where(head_lane, _softplus(sm_ref[...] + dtb_ref[...]), 0.0)
    a = dt * (-jnp.exp(alog_ref[...]))
    tri = _lower_tri(t, BF16)
    a_hi, a_lo = _split_hi_lo(a)
    a_cum = _dot(tri, a_hi) + _dot(tri, a_lo)
    a_cum_t = a_cum.T
    dt_t = dt.T
    total = a_cum[t - 1:t, :]
    e_a = jnp.exp(a_cum)
    w_state = dt * jnp.exp(total - a_cum)

    ex_hi, ex_lo = _split_hi_lo(jnp.concatenate([w_state, e_a], axis=0))
    e_mat = e_ref[...]
    ex = _dot(ex_hi, e_mat) + _dot(ex_lo, e_mat)
    w_state_x = ex[:t]
    e_a_x = ex[t:]
    chunk_decay_x = e_a_x[t - 1:t, :]

    xw = (xs * w_state_x).astype(BF16)
    state_prev = state_ref[...]
    state_prev_b = state_prev.astype(BF16)
    upd, y_off, cb = [], [], []
    for g in range(SSD_GROUPS):
        b_g = bc[:, g * n:(g + 1) * n].astype(BF16)
        c_g = bc[:, (SSD_GROUPS + g) * n:(SSD_GROUPS + g + 1) * n].astype(BF16)
        upd.append(_dot_tn(b_g, xw[:, g * gw:(g + 1) * gw]))
        y_off.append(_dot(c_g, state_prev_b[:, g * gw:(g + 1) * gw]))
        cb.append(_dot_nt(c_g, b_g))
    state_ref[...] = state_prev * chunk_decay_x + jnp.concatenate(upd, axis=1)
    y_off = jnp.concatenate(y_off, axis=1) * e_a_x

    quad = 4
    lane_c = lax.broadcasted_iota(jnp.int32, (1, SSD_INNER), 1)
    head_in_quad = (lane_c // SSD_HEAD_DIM) % quad
    xm = [jnp.where(head_in_quad == j, xs, 0.0).astype(BF16) for j in range(quad)]
    row = lax.broadcasted_iota(jnp.int32, (t, t), 0)
    col = lax.broadcasted_iota(jnp.int32, (t, t), 1)
    causal = row >= col
    qw = quad * SSD_HEAD_DIM
    y_quads = [None] * (SSD_HEADS // quad)
    for h in range(SSD_HEADS):
        g, q, j = h // hpg, h // quad, h % quad
        a_col = jnp.broadcast_to(a_cum[:, h:h + 1], (t, t))
        seg = jnp.where(causal, a_col - a_cum_t[h:h + 1, :], NEG_BIG)
        m = (cb[g] * jnp.exp(seg) * dt_t[h:h + 1, :]).astype(BF16)
        part = _dot(m, xm[j][:, q * qw:(q + 1) * qw])
        y_quads[q] = part if y_quads[q] is None else y_quads[q] + part
    y = jnp.concatenate(y_quads, axis=1) + y_off + xs * dsk_ref[...]

    y = y * gate_ref[...]
    parts = []
    for g in range(SSD_GROUPS):
        yg = y[:, g * gw:(g + 1) * gw]
        ms = jnp.mean(yg * yg, axis=-1, keepdims=True)
        parts.append(yg * lax.rsqrt(ms + EPS))
    y_ref[...] = (jnp.concatenate(parts, axis=1) * nw_ref[...]).astype(y_ref.dtype)


def _ssd_mixer(main, small, bsz, seqlen, cwx, cbx, dtb, alog, dsk, nw, e_mat):
    t = SSD_CHUNK
    nc = seqlen // t
    bc_w = 2 * SSD_GROUPS * SSD_STATE
    return pl.pallas_call(
        _ssd_kernel,
        grid=(bsz, nc),
        in_specs=[
            pl.BlockSpec((t, SSD_INNER), lambda b, c: (b * nc + c, COL_Z)),
            pl.BlockSpec((t, SSD_INNER), lambda b, c: (b * nc + c, COL_XS)),
            pl.BlockSpec((t, bc_w), lambda b, c: (b * nc + c, COL_BC)),
            pl.BlockSpec((t, LANES), lambda b, c: (b * nc + c, 0)),
            _const_spec((CONV_WIDTH, SSD_INNER)),
            _const_spec((1, SSD_INNER)),
            _const_spec((1, LANES)),
            _const_spec((1, LANES)),
            _const_spec((1, SSD_INNER)),
            _const_spec((1, SSD_INNER)),
            _const_spec((LANES, SSD_INNER)),
        ],
        out_specs=pl.BlockSpec((t, SSD_INNER), lambda b, c: (b * nc + c, 0)),
        out_shape=jax.ShapeDtypeStruct((bsz * seqlen, SSD_INNER), BF16),
        scratch_shapes=[pltpu.VMEM((SUBLANES, SSD_INNER), F32), pltpu.VMEM((SSD_STATE, SSD_INNER), F32)],
        compiler_params=pltpu.CompilerParams(
            dimension_semantics=("parallel", "arbitrary"), vmem_limit_bytes=VMEM_LIMIT_BYTES),
        name="ssd_mixer",
    )(main, main, main, small, cwx, cbx, dtb, alog, dsk, nw, e_mat)


def _l2norm_rows(x):
    return x * lax.rsqrt(jnp.sum(x * x, axis=-1, keepdims=True) + EPS)


GDN_STEP_CHUNKS = 4
GDN_PAIR = 2
GDN_QUAD = 4


def _pair_block_diag(m):
    lane = lax.broadcasted_iota(jnp.int32, m.shape, 1)
    zero = jnp.zeros((), m.dtype)
    return jnp.concatenate([jnp.where(lane < LANES, m, zero), jnp.where(lane >= LANES, m, zero)], axis=0)


def _packed_rows(rows8):
    n_pairs = GDN_HEADS // 2
    head = lax.broadcasted_iota(jnp.int32, rows8.shape, 0)
    m = jnp.where((head & 1) == 1, pltpu.roll(rows8, GDN_CHUNK, 1), rows8)
    m4 = jnp.concatenate([m] * n_pairs, axis=1)
    head4 = lax.broadcasted_iota(jnp.int32, m4.shape, 0)
    pair_of_lane = lax.broadcasted_iota(jnp.int32, m4.shape, 1) // LANES
    y = jnp.where((head4 >> 1) == pair_of_lane, m4, 0.0)
    for shift in (4, 2, 1):
        y = y + pltpu.roll(y, shift, 0)
    return jnp.concatenate([y] * (GDN_CHUNK // SUBLANES), axis=0)


def _gdn_kernel(q_ref, k_ref, v_ref, gate_ref, sm_ref, dtb_ref, alog_ref, nw_ref, e64_ref,
                o_ref, s_ref):
    t = GDN_CHUNK

    @pl.when(pl.program_id(1) == 0)
    def _init():
        s_ref[...] = jnp.zeros_like(s_ref)

    q = q_ref[...]
    k = k_ref[...]
    v_b = v_ref[...].astype(BF16)
    sm = sm_ref[...]
    gate = gate_ref[...]
    nw = nw_ref[...]
    e64 = e64_ref[...]
    nc = GDN_STEP_CHUNKS
    chunks = range(nc)
    pairs = range(GDN_HEADS // GDN_PAIR)
    quads = range(GDN_HEADS // GDN_QUAD)
    rows = [slice(c * t, (c + 1) * t) for c in chunks]
    pw = GDN_HEADS * t
    pair_w = GDN_PAIR * GDN_DK
    pair_pw = GDN_PAIR * t
    psl = [slice(p * pair_w, (p + 1) * pair_w) for p in pairs]
    ppl = [slice(p * pair_pw, (p + 1) * pair_pw) for p in pairs]

    qn, kn = [], []
    for h in range(GDN_HEADS):
        sl = slice(h * GDN_DK, (h + 1) * GDN_DK)
        qn.append(_l2norm_rows(q[:, sl]) * (GDN_DK ** -0.5))
        kn.append(_l2norm_rows(k[:, sl]))
    q_b = jnp.concatenate(qn, axis=1).astype(BF16)
    k_b = jnp.concatenate(kn, axis=1).astype(BF16)

    lane = lax.broadcasted_iota(jnp.int32, (1, LANES), 1)
    alpha_lane = (lane >= SM_ALPHA) & (lane < SM_ALPHA + GDN_HEADS)
    beta_lane = (lane >= SM_BETA) & (lane < SM_BETA + GDN_HEADS)
    beta = jnp.where(beta_lane, _sigmoid(sm), 0.0)
    g = jnp.where(alpha_lane, -jnp.exp(alog_ref[...]) * _softplus(sm + dtb_ref[...]), 0.0)
    row_s = lax.broadcasted_iota(jnp.int32, (nc * t, nc * t), 0)
    col_s = lax.broadcasted_iota(jnp.int32, (nc * t, nc * t), 1)
    tri = ((row_s >= col_s) & ((row_s // t) == (col_s // t))).astype(BF16)
    g_hi, g_lo = _split_hi_lo(g)
    gc_all = _dot(tri, g_hi) + _dot(tri, g_lo)
    e_gc_all = jnp.exp(gc_all)
    gc = [gc_all[r] for r in rows]
    e_gc = [e_gc_all[r] for r in rows]
    e_k = [jnp.exp(gc[c][t - 1:t, :] - gc[c]) for c in chunks]

    pad = jnp.zeros((LANES - t, LANES), F32)
    gc_rows8 = [jnp.concatenate([gc[c], pad], axis=0).T[SM_ALPHA:SM_ALPHA + GDN_HEADS, :] for c in chunks]
    beta_rows8 = [jnp.concatenate([beta[rows[c]], pad], axis=0).T[SM_BETA:SM_BETA + GDN_HEADS, :] for c in chunks]
    gc_rp = [_packed_rows(gc_rows8[c]) for c in chunks]
    beta_rp = [_packed_rows(beta_rows8[c]) for c in chunks]
    be_rp = [_packed_rows(beta_rows8[c] * jnp.exp(gc_rows8[c])) for c in chunks]
    gc_hi, gc_lo = _split_hi_lo(gc_all)
    ex = _dot(jnp.concatenate([gc_hi, gc_lo, beta.astype(BF16)], axis=0), e64)
    gc_cp_all = ex[:nc * t] + ex[nc * t:2 * nc * t]
    beta_cp_all = ex[2 * nc * t:]

    rowp = lax.broadcasted_iota(jnp.int32, (t, pw), 0)
    colp = lax.broadcasted_iota(jnp.int32, (t, pw), 1) & (t - 1)
    causal_p = colp <= rowp
    strict_p = colp < rowp
    decay_p = [jnp.exp(jnp.where(causal_p, gc_cp_all[rows[c]] - gc_rp[c], NEG_BIG)) for c in chunks]

    kq = [[_dot_nt(jnp.concatenate([k_b[rows[c], psl[p]], q_b[rows[c], psl[p]]], axis=0),
                   _pair_block_diag(k_b[rows[c], psl[p]])) for p in pairs] for c in chunks]
    amat = [jnp.where(strict_p, jnp.concatenate([kq[c][p][:t] for p in pairs], axis=1)
                      * beta_cp_all[rows[c]] * decay_p[c], 0.0) for c in chunks]
    qk_b = [jnp.where(causal_p, jnp.concatenate([kq[c][p][t:] for p in pairs], axis=1) * decay_p[c], 0.0).astype(BF16)
            for c in chunks]

    qd = GDN_QUAD * t
    blk_r = lax.broadcasted_iota(jnp.int32, (qd, qd), 0) // t
    blk_c = lax.broadcasted_iota(jnp.int32, (qd, qd), 1) // t
    same_block = blk_r == blk_c
    eye_q = ((lax.broadcasted_iota(jnp.int32, (t, qd), 1) & (t - 1))
             == lax.broadcasted_iota(jnp.int32, (t, qd), 0)).astype(F32)

    def block_diag4(m):
        return jnp.where(same_block, jnp.concatenate([m] * GDN_QUAD, axis=0), 0.0).astype(BF16)

    units = [(c, d) for c in chunks for d in quads]
    a_q = {u: amat[u[0]][:, u[1] * qd:(u[1] + 1) * qd] for u in units}
    x_q = {u: eye_q - a_q[u] for u in units}
    p_q = {u: _dot(a_q[u].astype(BF16), block_diag4(a_q[u])) for u in units}
    n_sq = (t - 1).bit_length() - 1
    for it in range(n_sq):
        last = it + 1 == n_sq
        xp = {u: _dot((x_q[u] if last else jnp.concatenate([x_q[u], p_q[u]], axis=0)).astype(BF16),
                      block_diag4(p_q[u])) for u in units}
        x_q = {u: x_q[u] + xp[u][:t] for u in units}
        if not last:
            p_q = {u: xp[u][t:] for u in units}
    t_inv = [jnp.concatenate([x_q[(c, d)] for d in quads], axis=1) for c in chunks]
    tb = [(t_inv[c] * beta_rp[c]).astype(BF16) for c in chunks]
    tbe = [(t_inv[c] * be_rp[c]).astype(BF16) for c in chunks]

    u_base = [jnp.concatenate([_dot(tb[c][:, ppl[p]], _pair_block_diag(v_b[rows[c], psl[p]])) for p in pairs], axis=1)
              for c in chunks]
    w_b = [[_dot(tbe[c][:, ppl[p]], _pair_block_diag(k_b[rows[c], psl[p]])).astype(BF16) for p in pairs]
           for c in chunks]

    def expand_cols(x, base):
        return jnp.concatenate(
            [jnp.broadcast_to(x[:, base + h:base + h + 1], (t, GDN_DV)) for h in range(GDN_HEADS)], axis=1)

    e_gc_x = [expand_cols(e_gc[c], SM_ALPHA) for c in chunks]
    e_k_x = [expand_cols(e_k[c], SM_ALPHA) for c in chunks]

    s_all = s_ref[...]
    o_chunks = []
    for c in chunks:
        s_b = s_all.astype(BF16)
        r = [_dot(jnp.concatenate([w_b[c][p], q_b[rows[c], psl[p]]], axis=0), _pair_block_diag(s_b[:, psl[p]]))
             for p in pairs]
        u = u_base[c] - jnp.concatenate([r[p][:t] for p in pairs], axis=1)
        u_b = u.astype(BF16)
        u_dec_b = (u * e_k_x[c]).astype(BF16)
        o2 = [_dot(qk_b[c][:, ppl[p]], _pair_block_diag(u_b[:, psl[p]])) for p in pairs]
        full = [_dot_tn(k_b[rows[c], psl[p]], u_dec_b[:, psl[p]]) for p in pairs]
        upd = jnp.concatenate([blk for p in pairs for blk in (full[p][:GDN_DK, :GDN_DV], full[p][GDN_DK:, GDN_DV:])],
                              axis=1)
        s_all = s_all * e_gc_x[c][t - 1:t, :] + upd
        o_chunks.append(jnp.concatenate([r[p][t:] for p in pairs], axis=1) * e_gc_x[c] + jnp.concatenate(o2, axis=1))
    s_ref[...] = s_all

    o = jnp.concatenate(o_chunks, axis=0)
    outs = []
    for h in range(GDN_HEADS):
        sl = slice(h * GDN_DV, (h + 1) * GDN_DV)
        oh = o[:, sl]
        oh = oh * lax.rsqrt(jnp.mean(oh * oh, axis=-1, keepdims=True) + EPS) * nw
        outs.append((oh * gate[:, sl]).astype(o_ref.dtype))
    o_ref[...] = jnp.concatenate(outs, axis=1)


def _gdn_mixer(main, small, bsz, seqlen, dtb, alog, nw, e64):
    t = GDN_STEP_CHUNKS * GDN_CHUNK
    nc = seqlen // t
    row_spec = lambda colblk: pl.BlockSpec((t, GDN_QK), lambda b, c: (b * nc + c, colblk))
    return pl.pallas_call(
        _gdn_kernel,
        grid=(bsz, nc),
        in_specs=[
            row_spec(COL_Q), row_spec(COL_K), row_spec(COL_V), row_spec(COL_GZ),
            pl.BlockSpec((t, LANES), lambda b, c: (b * nc + c, 0)),
            _const_spec((1, LANES)),
            _const_spec((1, LANES)),
            _const_spec((1, GDN_DV)),
            _const_spec((LANES, GDN_HEADS * GDN_CHUNK)),
        ],
        out_specs=pl.BlockSpec((t, GDN_VAL), lambda b, c: (b * nc + c, 0)),
        out_shape=jax.ShapeDtypeStruct((bsz * seqlen, GDN_VAL), BF16),
        scratch_shapes=[pltpu.VMEM((GDN_DK, GDN_HEADS * GDN_DV), F32)],
        compiler_params=pltpu.CompilerParams(
            dimension_semantics=("parallel", "arbitrary"), vmem_limit_bytes=VMEM_LIMIT_BYTES),
        name="gdn_mixer",
    )(main, main, main, main, small, dtb, alog, nw, e64)


SB_TILE = 256
SB_LANE_TILES = 8
SB_HEADS_PER_LANE_TILE = LANES // SB_HEAD_DIM


def _sb_kernel(q_ref, k_ref, v_ref, su_ref, o_ref, acc_ref, carry_ref):
    t = SB_TILE
    qi = pl.program_id(2)
    lane = lax.broadcasted_iota(jnp.int32, (1, LANES), 1)
    head_masks = [(lane // SB_HEAD_DIM) == p for p in range(SB_HEADS_PER_LANE_TILE)]
    zero_b = jnp.zeros((), BF16)
    q = q_ref[...]
    qm = [[jnp.where(hm, q[:, g * LANES:(g + 1) * LANES], zero_b) for hm in head_masks]
          for g in range(SB_LANE_TILES)]
    row = lax.broadcasted_iota(jnp.int32, (t, t), 0)
    col = lax.broadcasted_iota(jnp.int32, (t, t), 1)
    valid = col < row
    su = su_ref[...]

    acc_ref[...] = jnp.zeros_like(acc_ref)
    carry_ref[...] = jnp.zeros_like(carry_ref)

    def key_tile(j, diagonal):
        start = pl.multiple_of(j * t, t)
        k_all = k_ref[pl.ds(start, t), :]
        v_all = v_ref[pl.ds(start, t), :]
        carry_all = carry_ref[...]
        heads = [(g, p) for g in range(SB_LANE_TILES) for p in range(SB_HEADS_PER_LANE_TILE)]
        lanes_of = lambda g: slice(g * LANES, (g + 1) * LANES)
        s = [_dot_nt(qm[g][p], k_all[:, lanes_of(g)]) for g, p in heads]
        log_beta, log_1m = [], []
        for i in range(len(heads)):
            soft = jnp.log(1.0 + jnp.exp2(jnp.abs(s[i]) * (-LOG2_E)))
            lb = jnp.minimum(s[i], 0.0) - soft
            lm = lb - s[i]
            if diagonal:
                lm = jnp.where(valid, lm, 0.0)
            log_beta.append(lb)
            log_1m.append(lm)
        suffix = [_dot(lm.astype(BF16), su) for lm in log_1m]
        parts, carries = [], []
        for i, (g, p) in enumerate(heads):
            carry = carry_all[:, i * LANES:(i + 1) * LANES]
            att = jnp.exp(log_beta[i] + suffix[i] + jnp.concatenate([carry] * (t // LANES), axis=1))
            if diagonal:
                att = jnp.where(valid, att, 0.0)
            parts.append(_dot(att.astype(BF16), jnp.where(head_masks[p], v_all[:, lanes_of(g)], zero_b)))
            carries.append(carry + jnp.sum(log_1m[i], axis=-1, keepdims=True))
        outs = [functools.reduce(lambda a, b: a + b, parts[g * SB_HEADS_PER_LANE_TILE:(g + 1) * SB_HEADS_PER_LANE_TILE])
                for g in range(SB_LANE_TILES)]
        acc_ref[...] += jnp.concatenate(outs, axis=1)
        carry_ref[...] = jnp.concatenate(carries, axis=1)

    key_tile(qi, True)

    def body(i, c):
        key_tile(qi - 1 - i, False)
        return c

    lax.fori_loop(0, qi, body, 0)
    o_ref[...] = acc_ref[...].astype(o_ref.dtype)


def _sb_attention(qkv, bsz, seqlen, su):
    t = SB_TILE
    nq = seqlen // t
    w = SB_LANE_TILES * LANES
    n_groups = D_MODEL // w
    return pl.pallas_call(
        _sb_kernel,
        grid=(bsz, n_groups, nq),
        in_specs=[
            pl.BlockSpec((t, w), lambda b, p, i: (b * nq + i, p)),
            pl.BlockSpec((seqlen, w), lambda b, p, i: (b, n_groups + p)),
            pl.BlockSpec((seqlen, w), lambda b, p, i: (b, 2 * n_groups + p)),
            _const_spec((t, t)),
        ],
        out_specs=pl.BlockSpec((t, w), lambda b, p, i: (b * nq + i, p)),
        out_shape=jax.ShapeDtypeStruct((bsz * seqlen, D_MODEL), BF16),
        scratch_shapes=[
            pltpu.VMEM((t, w), F32),
            pltpu.VMEM((t, SB_LANE_TILES * SB_HEADS_PER_LANE_TILE * LANES), F32),
        ],
        compiler_params=pltpu.CompilerParams(
            dimension_semantics=("parallel", "parallel", "arbitrary"), vmem_limit_bytes=VMEM_LIMIT_BYTES),
        name="sb_attention",
    )(qkv, qkv, qkv, su)


MLP_FF_CHUNK = 512


def _post_kernel(n_y, x_ref, *refs):
    y_refs = refs[:n_y]
    wo_refs = refs[n_y:2 * n_y]
    nw_ref, w1_ref, w2_ref, o_ref, hid_ref = refs[2 * n_y:]
    x1 = x_ref[...]
    for y_ref, wo_ref in zip(y_refs, wo_refs):
        x1 = x1 + _dot(y_ref[...], wo_ref[...])
    h = _rmsnorm_rows(x1, nw_ref[...]).astype(BF16)
    for c in range(D_FF // MLP_FF_CHUNK):
        sl = slice(c * MLP_FF_CHUNK, (c + 1) * MLP_FF_CHUNK)
        a = jnp.maximum(_dot(h, w1_ref[:, sl]), 0.0)
        hid_ref[:, sl] = (a * a).astype(BF16)
    o_ref[...] = x1 + _dot(hid_ref[...], w2_ref[...])


def _post_block(x2, ys, wos, norm_w, w1, w2, tm=256):
    m = x2.shape[0]
    n_y = len(ys)
    row = lambda width: pl.BlockSpec((tm, width), lambda i: (i, 0))
    return pl.pallas_call(
        functools.partial(_post_kernel, n_y),
        grid=(m // tm,),
        in_specs=[row(D_MODEL)] + [row(y.shape[1]) for y in ys]
        + [_const_spec(w.shape) for w in wos]
        + [_const_spec((1, D_MODEL)), _const_spec((D_MODEL, D_FF)), _const_spec((D_FF, D_MODEL))],
        out_specs=row(D_MODEL),
        out_shape=jax.ShapeDtypeStruct((m, D_MODEL), F32),
        scratch_shapes=[pltpu.VMEM((tm, D_FF), BF16)],
        compiler_params=pltpu.CompilerParams(
            dimension_semantics=("parallel",), vmem_limit_bytes=VMEM_LIMIT_BYTES),
        name="post_block",
    )(x2, *ys, *wos, norm_w, w1, w2)


def _pad_lanes(vec, offset):
    out = jnp.zeros((1, LANES), F32)
    return out.at[0, offset:offset + vec.shape[0]].set(vec.astype(F32))


def _layer0_params(a_norm_w, a_w_in, ssd_conv_w, ssd_conv_b, ssd_dt_bias, ssd_a_log, ssd_d_skip,
                   ssd_norm_w, gdn_conv_w, gdn_a_log, gdn_dt_bias, gdn_norm_w):
    o_z = 0
    o_xbc = o_z + SSD_INNER
    o_dt = o_xbc + SSD_INNER + 2 * SSD_GROUPS * SSD_STATE
    o_qkv = o_dt + SSD_HEADS
    o_gz = o_qkv + 2 * GDN_QK + GDN_VAL
    o_beta = o_gz + GDN_VAL
    o_alpha = o_beta + GDN_HEADS
    w = a_w_in
    w_main = jnp.concatenate(
        [w[:, o_qkv:o_gz], w[:, o_z:o_xbc], w[:, o_gz:o_beta], w[:, o_xbc:o_dt]], axis=1).astype(BF16)
    w_small = jnp.concatenate(
        [w[:, o_dt:o_qkv], w[:, o_beta:o_alpha], w[:, o_alpha:o_alpha + GDN_HEADS],
         jnp.zeros((D_MODEL, LANES - SSD_HEADS - 2 * GDN_HEADS), w.dtype)], axis=1).astype(BF16)
    head_of_lane = jnp.arange(SSD_INNER) // SSD_HEAD_DIM
    e_mat = (jnp.arange(LANES)[:, None] == head_of_lane[None, :]).astype(BF16)
    head_of_packed = jnp.arange(GDN_HEADS * GDN_CHUNK) // GDN_CHUNK
    src = jnp.arange(LANES)[:, None]
    e64 = ((src == head_of_packed[None, :] + SM_BETA) | (src == head_of_packed[None, :] + SM_ALPHA)).astype(BF16)
    gate_w = GATE_COLS[1] - GATE_COLS[0]
    conv_w = jnp.concatenate([gdn_conv_w.astype(F32), jnp.zeros((CONV_WIDTH, gate_w), F32),
                              ssd_conv_w.astype(F32)], axis=1)
    conv_b = jnp.concatenate([jnp.zeros((GATE_COLS[1],), F32), ssd_conv_b.astype(F32)]).reshape(1, -1)
    conv_w = jnp.repeat(conv_w, SUBLANES, axis=0)
    conv_b = jnp.repeat(conv_b, SUBLANES, axis=0)
    return dict(
        norm_w=a_norm_w.reshape(1, D_MODEL), w_main=w_main, w_small=w_small, conv_w=conv_w, conv_b=conv_b,
        cwx=ssd_conv_w[:, :SSD_INNER].astype(F32), cbx=ssd_conv_b[:SSD_INNER].astype(F32).reshape(1, -1),
        ssd_dtb=_pad_lanes(ssd_dt_bias, SM_DT), ssd_alog=_pad_lanes(ssd_a_log, SM_DT),
        dsk=jnp.repeat(ssd_d_skip.astype(F32), SSD_HEAD_DIM).reshape(1, -1),
        ssd_nw=ssd_norm_w.reshape(1, -1), e_mat=e_mat,
        gdn_dtb=_pad_lanes(gdn_dt_bias, SM_ALPHA), gdn_alog=_pad_lanes(gdn_a_log, SM_ALPHA),
        gdn_nw=gdn_norm_w.reshape(1, -1), e64=e64)


def kernel(x, a_norm_w, a_w_in, ssd_conv_w, ssd_conv_b, ssd_dt_bias, ssd_a_log, ssd_d_skip, ssd_norm_w,
           gdn_conv_w, gdn_a_log, gdn_dt_bias, gdn_norm_w, a_w_out, c_norm_w, c_w_qkv, c_q_norm_w,
           c_k_norm_w, c_w_o, mlp_norm_w, mlp_w1, mlp_w2):
    bsz, seqlen, d = x.shape
    x2 = x.reshape(bsz * seqlen, d)

    p = _layer0_params(a_norm_w[0], a_w_in[0], ssd_conv_w[0], ssd_conv_b[0], ssd_dt_bias[0], ssd_a_log[0],
                       ssd_d_skip[0], ssd_norm_w[0], gdn_conv_w[0], gdn_a_log[0], gdn_dt_bias[0], gdn_norm_w[0])
    main, small = _in_proj(x2, seqlen, p["norm_w"], p["w_main"], p["w_small"], p["conv_w"], p["conv_b"])
    y_ssd = _ssd_mixer(main, small, bsz, seqlen, p["cwx"], p["cbx"], p["ssd_dtb"], p["ssd_alog"], p["dsk"],
                       p["ssd_nw"], p["e_mat"])
    y_gdn = _gdn_mixer(main, small, bsz, seqlen, p["gdn_dtb"], p["gdn_alog"], p["gdn_nw"], p["e64"])
    w_out = a_w_out[0].astype(BF16)
    x2 = _post_block(x2, [y_ssd, y_gdn], [w_out[:SSD_INNER], w_out[SSD_INNER:]],
                     mlp_norm_w[0].reshape(1, d), mlp_w1[0].astype(BF16), mlp_w2[0].astype(BF16))

    heads_per_chunk = PROJ_COL_CHUNK // SB_HEAD_DIM
    lane_head = jnp.arange(PROJ_COL_CHUNK) // SB_HEAD_DIM
    gmat = (lane_head[:, None] == lane_head[None, :]).astype(BF16)
    qw = jnp.tile(c_q_norm_w[0].astype(F32) * (SB_HEAD_DIM ** -0.5), heads_per_chunk).reshape(1, -1)
    kw = jnp.tile(c_k_norm_w[0].astype(F32), heads_per_chunk).reshape(1, -1)
    qkv = _qkv_proj(x2, c_norm_w[0].reshape(1, d), c_w_qkv[0].astype(BF16), gmat, qw, kw)
    key = jnp.arange(SB_TILE)
    su = (key[:, None] > key[None, :]).astype(BF16)
    o = _sb_attention(qkv, bsz, seqlen, su)
    x2 = _post_block(x2, [o], [c_w_o[0].astype(BF16)],
                     mlp_norm_w[1].reshape(1, d), mlp_w1[1].astype(BF16), mlp_w2[1].astype(BF16))
    return x2.reshape(bsz, seqlen, d)
```

```python
import functools

import jax
import jax.numpy as jnp
from jax import lax
from jax.experimental import pallas as pl
from jax.experimental.pallas import tpu as pltpu

F32 = jnp.float32
BF16 = jnp.bfloat16

EPS = 1e-6
D_MODEL = 1024
D_FF = 4 * D_MODEL
CONV_WIDTH = 4
SSD_HEADS = 16
SSD_HEAD_DIM = 64
SSD_INNER = SSD_HEADS * SSD_HEAD_DIM
SSD_GROUPS = 2
SSD_STATE = 128
SSD_CHUNK = 128
GDN_HEADS = 8
GDN_DK = 128
GDN_DV = 128
GDN_CHUNK = 64
GDN_QK = GDN_HEADS * GDN_DK
GDN_VAL = GDN_HEADS * GDN_DV
SB_HEADS = 16
SB_HEAD_DIM = D_MODEL // SB_HEADS
SB_BLOCK = 128

LANES = 128
SUBLANES = 8
VMEM_LIMIT_BYTES = 56 * 1024 * 1024

SM_DT = 0
SM_BETA = SSD_HEADS
SM_ALPHA = SSD_HEADS + GDN_HEADS
MAIN_WIDTH = 3 * GDN_QK + SSD_INNER + GDN_VAL + SSD_INNER + 2 * SSD_GROUPS * SSD_STATE
COL_Q, COL_K, COL_V, COL_Z, COL_GZ, COL_XS = 0, 1, 2, 3, 4, 5
COL_BC = (3 * GDN_QK + SSD_INNER + GDN_VAL + SSD_INNER) // (2 * SSD_GROUPS * SSD_STATE)

NEG_BIG = -1e30
LOG2_E = 1.4426950408889634


def _dot(a, b):
    return jnp.dot(a, b, preferred_element_type=F32)


def _dot_nt(a, b):
    return lax.dot_general(a, b, (((1,), (1,)), ((), ())), preferred_element_type=F32)


def _dot_tn(a, b):
    return lax.dot_general(a, b, (((0,), (0,)), ((), ())), preferred_element_type=F32)


def _split_hi_lo(v):
    hi = v.astype(BF16)
    lo = (v - hi.astype(F32)).astype(BF16)
    return hi, lo


def _softplus(x):
    return jnp.maximum(x, 0.0) + jnp.log1p(jnp.exp(-jnp.abs(x)))


def _sigmoid(x):
    return 1.0 / (1.0 + jnp.exp(-x))


def _silu(x):
    return x * _sigmoid(x)


def _lower_tri(n, dtype):
    row = lax.broadcasted_iota(jnp.int32, (n, n), 0)
    col = lax.broadcasted_iota(jnp.int32, (n, n), 1)
    return (row >= col).astype(dtype)


def _tile_rows(a8, rows):
    return jnp.concatenate([a8] * (rows // SUBLANES), axis=0)


def _row_to_tile(row, rows):
    return _tile_rows(jnp.broadcast_to(row, (SUBLANES, row.shape[1])), rows)


def _conv_silu(x, tail, w8, b8):
    t, c = x.shape
    tap = lambda k: _tile_rows(w8[k * SUBLANES:(k + 1) * SUBLANES, :], t)
    acc = x * tap(CONV_WIDTH - 1) + _tile_rows(b8, t)
    row8 = lax.broadcasted_iota(jnp.int32, (SUBLANES, c), 0)
    for k in range(1, CONV_WIDTH):
        r = pltpu.roll(x, k, 0)
        tr = pltpu.roll(tail, k, 0)
        first = jnp.where(row8 < k, tr, r[0:SUBLANES])
        shifted = jnp.concatenate([first, r[SUBLANES:]], axis=0)
        acc = acc + shifted * tap(CONV_WIDTH - 1 - k)
    return _silu(acc), x[t - SUBLANES:t]


PROJ_COL_CHUNK = 512
IN_PROJ_COL_CHUNK = 256


def _rmsnorm_rows(x, w):
    ms = jnp.mean(x * x, axis=-1, keepdims=True)
    return x * lax.rsqrt(ms + EPS) * w


GATE_COLS = (COL_Z * GDN_QK, (COL_GZ + 1) * GDN_QK)
RAW_COLS = (COL_XS * SSD_INNER, (COL_XS + 1) * SSD_INNER)


def _in_proj_kernel(tiles_per_seq, x_ref, nw_ref, w_ref, ws_ref, cw_ref, cb_ref, main_ref, small_ref,
                    tail_ref, win_ref):
    tm = x_ref.shape[0]

    @pl.when(pl.program_id(0) % tiles_per_seq == 0)
    def _init():
        tail_ref[...] = jnp.zeros_like(tail_ref)

    h = _rmsnorm_rows(x_ref[...], nw_ref[...]).astype(BF16)
    small_ref[...] = _dot(h, ws_ref[...])
    for c in range(MAIN_WIDTH // IN_PROJ_COL_CHUNK):
        sl = slice(c * IN_PROJ_COL_CHUNK, (c + 1) * IN_PROJ_COL_CHUNK)
        y = _dot(h, w_ref[:, sl])
        if GATE_COLS[0] <= sl.start < GATE_COLS[1]:
            main_ref[:, sl] = _silu(y)
            continue
        if RAW_COLS[0] <= sl.start < RAW_COLS[1]:
            main_ref[:, sl] = y
            continue
        rep = lambda a: jnp.concatenate([a] * (tm // SUBLANES), axis=0)
        w0, w1, w2, w3 = (rep(cw_ref[k * SUBLANES:(k + 1) * SUBLANES, sl]) for k in range(CONV_WIDTH))
        bias = rep(cb_ref[:, sl])
        win_y = win_ref.at[0, c % 2]
        win_u = win_ref.at[1, c % 2]
        win_y[0:SUBLANES, :] = tail_ref[0, :, sl]
        win_y[SUBLANES:, :] = y
        tail_ref[0, :, sl] = y[tm - SUBLANES:tm]
        y1 = win_y[pl.ds(SUBLANES - 1, tm), :]
        u = y * w1 + y1 * w0
        win_u[0:SUBLANES, :] = tail_ref[1, :, sl]
        win_u[SUBLANES:, :] = u
        tail_ref[1, :, sl] = u[tm - SUBLANES:tm]
        acc = y * w3 + y1 * w2 + win_u[pl.ds(SUBLANES - 2, tm), :] + bias
        main_ref[:, sl] = _silu(acc)


def _const_spec(shape):
    nd = len(shape)
    return pl.BlockSpec(shape, lambda *_: (0,) * nd, pipeline_mode=pl.Buffered(1))


def _in_proj(x2, seqlen, norm_w, w_main, w_small, conv_w, conv_b, tm=256):
    m = x2.shape[0]
    return pl.pallas_call(
        functools.partial(_in_proj_kernel, seqlen // tm),
        grid=(m // tm,),
        in_specs=[
            pl.BlockSpec((tm, D_MODEL), lambda i: (i, 0)),
            _const_spec((1, D_MODEL)),
            _const_spec((D_MODEL, MAIN_WIDTH)),
            _const_spec((D_MODEL, LANES)),
            _const_spec((CONV_WIDTH * SUBLANES, MAIN_WIDTH)),
            _const_spec((SUBLANES, MAIN_WIDTH)),
        ],
        out_specs=[
            pl.BlockSpec((tm, MAIN_WIDTH), lambda i: (i, 0)),
            pl.BlockSpec((tm, LANES), lambda i: (i, 0)),
        ],
        out_shape=[
            jax.ShapeDtypeStruct((m, MAIN_WIDTH), F32),
            jax.ShapeDtypeStruct((m, LANES), F32),
        ],
        scratch_shapes=[pltpu.VMEM((2, SUBLANES, MAIN_WIDTH), F32),
                        pltpu.VMEM((2, 2, SUBLANES + tm, IN_PROJ_COL_CHUNK), F32)],
        compiler_params=pltpu.CompilerParams(
            dimension_semantics=("arbitrary",), vmem_limit_bytes=VMEM_LIMIT_BYTES),
        name="in_proj",
    )(x2, norm_w, w_main, w_small, conv_w, conv_b)


def _qkv_proj_kernel(x_ref, nw_ref, w_ref, gmat_ref, qw_ref, kw_ref, o_ref):
    h = _rmsnorm_rows(x_ref[...], nw_ref[...]).astype(BF16)
    gmat = gmat_ref[...]
    n_chunks = D_MODEL // PROJ_COL_CHUNK
    for part, hw_ref in ((0, qw_ref), (1, kw_ref)):
        for c in range(n_chunks):
            col = part * D_MODEL + c * PROJ_COL_CHUNK
            y = _dot(h, w_ref[:, col:col + PROJ_COL_CHUNK])
            ms = _dot((y * y).astype(BF16), gmat) * (1.0 / SB_HEAD_DIM)
            o_ref[:, col:col + PROJ_COL_CHUNK] = (y * lax.rsqrt(ms + EPS) * hw_ref[...]).astype(o_ref.dtype)
    for c in range(n_chunks):
        col = 2 * D_MODEL + c * PROJ_COL_CHUNK
        o_ref[:, col:col + PROJ_COL_CHUNK] = _dot(h, w_ref[:, col:col + PROJ_COL_CHUNK]).astype(o_ref.dtype)


def _qkv_proj(x2, norm_w, w_qkv, gmat, qw, kw, tm=512):
    m = x2.shape[0]
    return pl.pallas_call(
        _qkv_proj_kernel,
        grid=(m // tm,),
        in_specs=[
            pl.BlockSpec((tm, D_MODEL), lambda i: (i, 0)),
            _const_spec((1, D_MODEL)),
            _const_spec((D_MODEL, 3 * D_MODEL)),
            _const_spec((PROJ_COL_CHUNK, PROJ_COL_CHUNK)),
            _const_spec((1, PROJ_COL_CHUNK)),
            _const_spec((1, PROJ_COL_CHUNK)),
        ],
        out_specs=pl.BlockSpec((tm, 3 * D_MODEL), lambda i: (i, 0)),
        out_shape=jax.ShapeDtypeStruct((m, 3 * D_MODEL), BF16),
        compiler_params=pltpu.CompilerParams(
            dimension_semantics=("parallel",), vmem_limit_bytes=VMEM_LIMIT_BYTES),
        name="qkv_proj",
    )(x2, norm_w, w_qkv, gmat, qw, kw)


SSD_STEP_CHUNKS = 4


def _ssd_kernel(gate_ref, xs_ref, bc_ref, sm_ref, cwx_ref, cbx_ref, dtb_ref, alog_ref, dsk_ref, nw_ref, e_ref,
                y_ref, tailx_ref, state_ref):
    t = SSD_CHUNK
    n = SSD_STATE
    hpg = SSD_HEADS // SSD_GROUPS
    gw = hpg * SSD_HEAD_DIM
    chunks = range(SSD_STEP_CHUNKS)
    rows = [slice(c * t, (c + 1) * t) for c in chunks]
    groups = range(SSD_GROUPS)

    @pl.when(pl.program_id(1) == 0)
    def _init():
        tailx_ref[...] = jnp.zeros_like(tailx_ref)
        state_ref[...] = jnp.zeros_like(state_ref)

    xs_all, tailx_ref[...] = _conv_silu(xs_ref[...], tailx_ref[...], cwx_ref[...], cbx_ref[...])
    bc_b = bc_ref[...].astype(BF16)
    gate = gate_ref[...]
    dsk = _tile_rows(dsk_ref[...], t)
    nw = _tile_rows(nw_ref[...], t)

    lane = lax.broadcasted_iota(jnp.int32, (1, LANES), 1)
    head_lane = (lane >= SM_DT) & (lane < SM_DT + SSD_HEADS)
    dt_all = jnp.where(head_lane, _softplus(sm_ref[...] + dtb_ref[...]), 0.0)
    a_all = dt_all * (-jnp.exp(alog_ref[...]))
    tri = _lower_tri(t, BF16)
    a_cum, dt = [], []
    for c in chunks:
        a_hi, a_lo = _split_hi_lo(a_all[rows[c]])
        a_cum.append(_dot(tri, a_hi) + _dot(tri, a_lo))
        dt.append(dt_all[rows[c]])
    a_cum_t = [a_cum[c].T for c in chunks]
    dt_t = [dt[c].T for c in chunks]
    e_a = [jnp.exp(a_cum[c]) for c in chunks]
    w_state = [dt[c] * jnp.exp(a_cum[c][t - 1:t, :] - a_cum[c]) for c in chunks]

    e_mat = e_ref[...]
    ex = []
    for c in chunks:
        ex_hi, ex_lo = _split_hi_lo(jnp.concatenate([w_state[c], e_a[c]], axis=0))
        ex.append(_dot(ex_hi, e_mat) + _dot(ex_lo, e_mat))
    w_state_x = [ex[c][:t] for c in chunks]
    e_a_x = [ex[c][t:] for c in chunks]

    xs = [xs_all[rows[c]] for c in chunks]
    xw = [(xs[c] * w_state_x[c]).astype(BF16) for c in chunks]
    b_g = [[bc_b[rows[c], g * n:(g + 1) * n] for g in groups] for c in chunks]
    c_g = [[bc_b[rows[c], (SSD_GROUPS + g) * n:(SSD_GROUPS + g + 1) * n] for g in groups] for c in chunks]
    cb = [[_dot_nt(c_g[c][g], b_g[c][g]) for g in groups] for c in chunks]
    upd = [jnp.concatenate([_dot_tn(b_g[c][g], xw[c][:, g * gw:(g + 1) * gw]) for g in groups], axis=1)
           for c in chunks]

    state = state_ref[...]
    y_off = []
    for c in chunks:
        state_b = state.astype(BF16)
        y_off.append(jnp.concatenate([_dot(c_g[c][g], state_b[:, g * gw:(g + 1) * gw]) for g in groups], axis=1)
                     * e_a_x[c])
        state = state * _row_to_tile(e_a_x[c][t - 1:t, :], n) + upd[c]
    state_ref[...] = state

    row = lax.broadcasted_iota(jnp.int32, (t, t), 0)
    col = lax.broadcasted_iota(jnp.int32, (t, t), 1)
    causal = row >= col
    xs_b = [xs[c].astype(BF16) for c in chunks]
    y_heads = [[None] * SSD_HEADS for _ in chunks]
    for h in range(SSD_HEADS):
        g = h // hpg
        for c in chunks:
            a_col = jnp.broadcast_to(a_cum[c][:, h:h + 1], (t, t))
            seg = jnp.where(causal, a_col - _row_to_tile(a_cum_t[c][h:h + 1, :], t), NEG_BIG)
            m = (cb[c][g] * jnp.exp(seg) * _row_to_tile(dt_t[c][h:h + 1, :], t)).astype(BF16)
            y_heads[c][h] = _dot(m, xs_b[c][:, h * SSD_HEAD_DIM:(h + 1) * SSD_HEAD_DIM])

    outs = []
    for c in chunks:
        y = jnp.concatenate(y_heads[c], axis=1) + y_off[c] + xs[c] * dsk
        y = y * gate[rows[c]]
        parts = []
        for g in groups:
            yg = y[:, g * gw:(g + 1) * gw]
            ms = jnp.mean(yg * yg, axis=-1, keepdims=True)
            parts.append(yg * lax.rsqrt(ms + EPS))
        outs.append((jnp.concatenate(parts, axis=1) * nw).astype(y_ref.dtype))
    y_ref[...] = jnp.concatenate(outs, axis=0)


def _ssd_mixer(main, small, bsz, seqlen, cwx, cbx, dtb, alog, dsk, nw, e_mat):
    t = SSD_STEP_CHUNKS * SSD_CHUNK
    nc = seqlen // t
    bc_w = 2 * SSD_GROUPS * SSD_STATE
    return pl.pallas_call(
        _ssd_kernel,
        grid=(bsz, nc),
        in_specs=[
            pl.BlockSpec((t, SSD_INNER), lambda b, c: (b * nc + c, COL_Z)),
            pl.BlockSpec((t, SSD_INNER), lambda b, c: (b * nc + c, COL_XS)),
            pl.BlockSpec((t, bc_w), lambda b, c: (b * nc + c, COL_BC)),
            pl.BlockSpec((t, LANES), lambda b, c: (b * nc + c, 0)),
            _const_spec((CONV_WIDTH * SUBLANES, SSD_INNER)),
            _const_spec((SUBLANES, SSD_INNER)),
            _const_spec((1, LANES)),
            _const_spec((1, LANES)),
            _const_spec((SUBLANES, SSD_INNER)),
            _const_spec((SUBLANES, SSD_INNER)),
            _const_spec((LANES, SSD_INNER)),
        ],
        out_specs=pl.BlockSpec((t, SSD_INNER), lambda b, c: (b * nc + c, 0)),
        out_shape=jax.ShapeDtypeStruct((bsz * seqlen, SSD_INNER), BF16),
        scratch_shapes=[pltpu.VMEM((SUBLANES, SSD_INNER), F32), pltpu.VMEM((SSD_STATE, SSD_INNER), F32)],
        compiler_params=pltpu.CompilerParams(
            dimension_semantics=("parallel", "arbitrary"), vmem_limit_bytes=VMEM_LIMIT_BYTES),
        name="ssd_mixer",
    )(main, main, main, small, cwx, cbx, dtb, alog, dsk, nw, e_mat)


def _l2norm_rows(x):
    return x * lax.rsqrt(jnp.sum(x * x, axis=-1, keepdims=True) + EPS)


GDN_STEP_CHUNKS = 8
GDN_PAIR = 2
GDN_QUAD = 4


def _pair_block_diag(m):
    lane = lax.broadcasted_iota(jnp.int32, m.shape, 1)
    zero = jnp.zeros((), m.dtype)
    return jnp.concatenate([jnp.where(lane < LANES, m, zero), jnp.where(lane >= LANES, m, zero)], axis=0)


def _packed_rows(rows8):
    n_pairs = GDN_HEADS // 2
    head = lax.broadcasted_iota(jnp.int32, rows8.shape, 0)
    m = jnp.where((head & 1) == 1, pltpu.roll(rows8, GDN_CHUNK, 1), rows8)
    m4 = jnp.concatenate([m] * n_pairs, axis=1)
    head4 = lax.broadcasted_iota(jnp.int32, m4.shape, 0)
    pair_of_lane = lax.broadcasted_iota(jnp.int32, m4.shape, 1) // LANES
    y = jnp.where((head4 >> 1) == pair_of_lane, m4, 0.0)
    for shift in (4, 2, 1):
        y = y + pltpu.roll(y, shift, 0)
    return jnp.concatenate([y] * (GDN_CHUNK // SUBLANES), axis=0)


def _gdn_kernel(q_ref, k_ref, v_ref, gate_ref, sm_ref, dtb_ref, alog_ref, nw_ref, e64_ref,
                o_ref, s_ref):
    t = GDN_CHUNK

    @pl.when(pl.program_id(1) == 0)
    def _init():
        s_ref[...] = jnp.zeros_like(s_ref)

    q = q_ref[...]
    k = k_ref[...]
    v_b = v_ref[...].astype(BF16)
    sm = sm_ref[...]
    gate = gate_ref[...]
    nw = nw_ref[...]
    e64 = e64_ref[...]
    nc = GDN_STEP_CHUNKS
    chunks = range(nc)
    pairs = range(GDN_HEADS // GDN_PAIR)
    quads = range(GDN_HEADS // GDN_QUAD)
    rows = [slice(c * t, (c + 1) * t) for c in chunks]
    pw = GDN_HEADS * t
    pair_w = GDN_PAIR * GDN_DK
    pair_pw = GDN_PAIR * t
    psl = [slice(p * pair_w, (p + 1) * pair_w) for p in pairs]
    ppl = [slice(p * pair_pw, (p + 1) * pair_pw) for p in pairs]

    qn, kn = [], []
    for h in range(GDN_HEADS):
        sl = slice(h * GDN_DK, (h + 1) * GDN_DK)
        qn.append(_l2norm_rows(q[:, sl]) * (GDN_DK ** -0.5))
        kn.append(_l2norm_rows(k[:, sl]))
    q_b = jnp.concatenate(qn, axis=1).astype(BF16)
    k_b = jnp.concatenate(kn, axis=1).astype(BF16)

    lane = lax.broadcasted_iota(jnp.int32, (1, LANES), 1)
    alpha_lane = (lane >= SM_ALPHA) & (lane < SM_ALPHA + GDN_HEADS)
    beta_lane = (lane >= SM_BETA) & (lane < SM_BETA + GDN_HEADS)
    beta = jnp.where(beta_lane, _sigmoid(sm), 0.0)
    g = jnp.where(alpha_lane, -jnp.exp(alog_ref[...]) * _softplus(sm + dtb_ref[...]), 0.0)
    row_s = lax.broadcasted_iota(jnp.int32, (nc * t, nc * t), 0)
    col_s = lax.broadcasted_iota(jnp.int32, (nc * t, nc * t), 1)
    tri = ((row_s >= col_s) & ((row_s // t) == (col_s // t))).astype(BF16)
    g_hi, g_lo = _split_hi_lo(g)
    gc_all = _dot(tri, g_hi) + _dot(tri, g_lo)
    e_gc_all = jnp.exp(gc_all)
    gc = [gc_all[r] for r in rows]
    e_gc = [e_gc_all[r] for r in rows]
    e_k = [jnp.exp(gc[c][t - 1:t, :] - gc[c]) for c in chunks]

    pad = jnp.zeros((LANES - t, LANES), F32)
    gc_rows8 = [jnp.concatenate([gc[c], pad], axis=0).T[SM_ALPHA:SM_ALPHA + GDN_HEADS, :] for c in chunks]
    beta_rows8 = [jnp.concatenate([beta[rows[c]], pad], axis=0).T[SM_BETA:SM_BETA + GDN_HEADS, :] for c in chunks]
    gc_rp = [_packed_rows(gc_rows8[c]) for c in chunks]
    beta_rp = [_packed_rows(beta_rows8[c]) for c in chunks]
    be_rp = [_packed_rows(beta_rows8[c] * jnp.exp(gc_rows8[c])) for c in chunks]
    gc_hi, gc_lo = _split_hi_lo(gc_all)
    ex = _dot(jnp.concatenate([gc_hi, gc_lo, beta.astype(BF16)], axis=0), e64)
    gc_cp_all = ex[:nc * t] + ex[nc * t:2 * nc * t]
    beta_cp_all = ex[2 * nc * t:]

    rowp = lax.broadcasted_iota(jnp.int32, (t, pw), 0)
    colp = lax.broadcasted_iota(jnp.int32, (t, pw), 1) & (t - 1)
    causal_p = colp <= rowp
    strict_p = colp < rowp
    decay_p = [jnp.exp(jnp.where(causal_p, gc_cp_all[rows[c]] - gc_rp[c], NEG_BIG)) for c in chunks]

    kq = [[_dot_nt(jnp.concatenate([k_b[rows[c], psl[p]], q_b[rows[c], psl[p]]], axis=0),
                   _pair_block_diag(k_b[rows[c], psl[p]])) for p in pairs] for c in chunks]
    amat = [jnp.where(strict_p, jnp.concatenate([kq[c][p][:t] for p in pairs], axis=1)
                      * beta_cp_all[rows[c]] * decay_p[c], 0.0) for c in chunks]
    qk_b = [jnp.where(causal_p, jnp.concatenate([kq[c][p][t:] for p in pairs], axis=1) * decay_p[c], 0.0).astype(BF16)
            for c in chunks]

    qd = GDN_QUAD * t
    blk_r = lax.broadcasted_iota(jnp.int32, (qd, qd), 0) // t
    blk_c = lax.broadcasted_iota(jnp.int32, (qd, qd), 1) // t
    same_block = blk_r == blk_c
    eye_q = ((lax.broadcasted_iota(jnp.int32, (t, qd), 1) & (t - 1))
             == lax.broadcasted_iota(jnp.int32, (t, qd), 0)).astype(F32)

    def block_diag4(m):
        return jnp.where(same_block, jnp.concatenate([m] * GDN_QUAD, axis=0), 0.0).astype(BF16)

    units = [(c, d) for c in chunks for d in quads]
    a_q = {u: amat[u[0]][:, u[1] * qd:(u[1] + 1) * qd] for u in units}
    x_q = {u: eye_q - a_q[u] for u in units}
    p_q = {u: _dot(a_q[u].astype(BF16), block_diag4(a_q[u])) for u in units}
    n_sq = (t - 1).bit_length() - 1
    for it in range(n_sq):
        last = it + 1 == n_sq
        xp = {u: _dot((x_q[u] if last else jnp.concatenate([x_q[u], p_q[u]], axis=0)).astype(BF16),
                      block_diag4(p_q[u])) for u in units}
        x_q = {u: x_q[u] + xp[u][:t] for u in units}
        if not last:
            p_q = {u: xp[u][t:] for u in units}
    t_inv = [jnp.concatenate([x_q[(c, d)] for d in quads], axis=1) for c in chunks]
    tb = [(t_inv[c] * beta_rp[c]).astype(BF16) for c in chunks]
    tbe = [(t_inv[c] * be_rp[c]).astype(BF16) for c in chunks]

    u_base = [jnp.concatenate([_dot(tb[c][:, ppl[p]], _pair_block_diag(v_b[rows[c], psl[p]])) for p in pairs], axis=1)
              for c in chunks]
    w_b = [[_dot(tbe[c][:, ppl[p]], _pair_block_diag(k_b[rows[c], psl[p]])).astype(BF16) for p in pairs]
           for c in chunks]

    def expand_cols(x, base):
        return jnp.concatenate(
            [jnp.broadcast_to(x[:, base + h:base + h + 1], (t, GDN_DV)) for h in range(GDN_HEADS)], axis=1)

    e_gc_x = [expand_cols(e_gc[c], SM_ALPHA) for c in chunks]
    e_k_x = [expand_cols(e_k[c], SM_ALPHA) for c in chunks]

    s_all = s_ref[...]
    o_chunks = []
    for c in chunks:
        s_b = s_all.astype(BF16)
        r = [_dot(jnp.concatenate([w_b[c][p], q_b[rows[c], psl[p]]], axis=0), _pair_block_diag(s_b[:, psl[p]]))
             for p in pairs]
        u = u_base[c] - jnp.concatenate([r[p][:t] for p in pairs], axis=1)
        u_b = u.astype(BF16)
        u_dec_b = (u * e_k_x[c]).astype(BF16)
        o2 = [_dot(qk_b[c][:, ppl[p]], _pair_block_diag(u_b[:, psl[p]])) for p in pairs]
        full = [_dot_tn(k_b[rows[c], psl[p]], u_dec_b[:, psl[p]]) for p in pairs]
        upd = jnp.concatenate([blk for p in pairs for blk in (full[p][:GDN_DK, :GDN_DV], full[p][GDN_DK:, GDN_DV:])],
                              axis=1)
        s_all = s_all * e_gc_x[c][t - 1:t, :] + upd
        o_chunks.append(jnp.concatenate([r[p][t:] for p in pairs], axis=1) * e_gc_x[c] + jnp.concatenate(o2, axis=1))
    s_ref[...] = s_all

    o = jnp.concatenate(o_chunks, axis=0)
    outs = []
    for h in range(GDN_HEADS):
        sl = slice(h * GDN_DV, (h + 1) * GDN_DV)
        oh = o[:, sl]
        oh = oh * lax.rsqrt(jnp.mean(oh * oh, axis=-1, keepdims=True) + EPS) * nw
        outs.append((oh * gate[:, sl]).astype(o_ref.dtype))
    o_ref[...] = jnp.concatenate(outs, axis=1)


def _gdn_mixer(main, small, bsz, seqlen, dtb, alog, nw, e64):
    t = GDN_STEP_CHUNKS * GDN_CHUNK
    nc = seqlen // t
    row_spec = lambda colblk: pl.BlockSpec((t, GDN_QK), lambda b, c: (b * nc + c, colblk))
    return pl.pallas_call(
        _gdn_kernel,
        grid=(bsz, nc),
        in_specs=[
            row_spec(COL_Q), row_spec(COL_K), row_spec(COL_V), row_spec(COL_GZ),
            pl.BlockSpec((t, LANES), lambda b, c: (b * nc + c, 0)),
            _const_spec((1, LANES)),
            _const_spec((1, LANES)),
            _const_spec((1, GDN_DV)),
            _const_spec((LANES, GDN_HEADS * GDN_CHUNK)),
        ],
        out_specs=pl.BlockSpec((t, GDN_VAL), lambda b, c: (b * nc + c, 0)),
        out_shape=jax.ShapeDtypeStruct((bsz * seqlen, GDN_VAL), BF16),
        scratch_shapes=[pltpu.VMEM((GDN_DK, GDN_HEADS * GDN_DV), F32)],
        compiler_params=pltpu.CompilerParams(
            dimension_semantics=("parallel", "arbitrary"), vmem_limit_bytes=VMEM_LIMIT_BYTES),
        name="gdn_mixer",
    )(main, main, main, main, small, dtb, alog, nw, e64)


SB_TILE = 256
SB_LANE_TILES = 8
SB_HEADS_PER_LANE_TILE = LANES // SB_HEAD_DIM


def _sb_kernel(q_ref, k_ref, v_ref, su_ref, o_ref, acc_ref, carry_ref):
    t = SB_TILE
    qi = pl.program_id(2)
    lane = lax.broadcasted_iota(jnp.int32, (1, LANES), 1)
    head_masks = [(lane // SB_HEAD_DIM) == p for p in range(SB_HEADS_PER_LANE_TILE)]
    zero_b = jnp.zeros((), BF16)
    q = q_ref[...]
    qm = [[jnp.where(hm, q[:, g * LANES:(g + 1) * LANES], zero_b) for hm in head_masks]
          for g in range(SB_LANE_TILES)]
    row = lax.broadcasted_iota(jnp.int32, (t, t), 0)
    col = lax.broadcasted_iota(jnp.int32, (t, t), 1)
    valid = col < row
    su = su_ref[...]

    acc_ref[...] = jnp.zeros_like(acc_ref)
    carry_ref[...] = jnp.zeros_like(carry_ref)

    def key_tile(j, diagonal):
        start = pl.multiple_of(j * t, t)
        k_all = k_ref[pl.ds(start, t), :]
        v_all = v_ref[pl.ds(start, t), :]
        carry_all = carry_ref[...]
        heads = [(g, p) for g in range(SB_LANE_TILES) for p in range(SB_HEADS_PER_LANE_TILE)]
        lanes_of = lambda g: slice(g * LANES, (g + 1) * LANES)
        s = [_dot_nt(qm[g][p], k_all[:, lanes_of(g)]) for g, p in heads]
        log_beta, log_1m = [], []
        for i in range(len(heads)):
            soft = jnp.log(1.0 + jnp.exp2(jnp.abs(s[i]) * (-LOG2_E)))
            lb = jnp.minimum(s[i], 0.0) - soft
            lm = lb - s[i]
            if diagonal:
                lm = jnp.where(valid, lm, 0.0)
            log_beta.append(lb)
            log_1m.append(lm)
        suffix = [_dot(lm.astype(BF16), su) for lm in log_1m]
        parts, carries = [], []
        for i, (g, p) in enumerate(heads):
            carry = carry_all[:, i * LANES:(i + 1) * LANES]
            att = jnp.exp(log_beta[i] + suffix[i] + jnp.concatenate([carry] * (t // LANES), axis=1))
            if diagonal:
                att = jnp.where(valid, att, 0.0)
            parts.append(_dot(att.astype(BF16), jnp.where(head_masks[p], v_all[:, lanes_of(g)], zero_b)))
            carries.append(carry + jnp.sum(log_1m[i], axis=-1, keepdims=True))
        outs = [functools.reduce(lambda a, b: a + b, parts[g * SB_HEADS_PER_LANE_TILE:(g + 1) * SB_HEADS_PER_LANE_TILE])
                for g in range(SB_LANE_TILES)]
        acc_ref[...] += jnp.concatenate(outs, axis=1)
        carry_ref[...] = jnp.concatenate(carries, axis=1)

    key_tile(qi, True)

    def body(i, c):
        key_tile(qi - 1 - i, False)
        return c

    lax.fori_loop(0, qi, body, 0)
    o_ref[...] = acc_ref[...].astype(o_ref.dtype)


def _sb_attention(qkv, bsz, seqlen, su):
    t = SB_TILE
    nq = seqlen // t
    w = SB_LANE_TILES * LANES
    n_groups = D_MODEL // w
    return pl.pallas_call(
        _sb_kernel,
        grid=(bsz, n_groups, nq),
        in_specs=[
            pl.BlockSpec((t, w), lambda b, p, i: (b * nq + i, p)),
            pl.BlockSpec((seqlen, w), lambda b, p, i: (b, n_groups + p)),
            pl.BlockSpec((seqlen, w), lambda b, p, i: (b, 2 * n_groups + p)),
            _const_spec((t, t)),
        ],
        out_specs=pl.BlockSpec((t, w), lambda b, p, i: (b * nq + i, p)),
        out_shape=jax.ShapeDtypeStruct((bsz * seqlen, D_MODEL), BF16),
        scratch_shapes=[
            pltpu.VMEM((t, w), F32),
            pltpu.VMEM((t, SB_LANE_TILES * SB_HEADS_PER_LANE_TILE * LANES), F32),
        ],
        compiler_params=pltpu.CompilerParams(
            dimension_semantics=("parallel", "parallel", "arbitrary"), vmem_limit_bytes=VMEM_LIMIT_BYTES),
        name="sb_attention",
    )(qkv, qkv, qkv, su)


MLP_FF_CHUNK = 512


def _post_kernel(n_y, x_ref, *refs):
    y_refs = refs[:n_y]
    wo_refs = refs[n_y:2 * n_y]
    nw_ref, w1_ref, w2_ref, o_ref, hid_ref = refs[2 * n_y:]
    x1 = x_ref[...]
    for y_ref, wo_ref in zip(y_refs, wo_refs):
        x1 = x1 + _dot(y_ref[...], wo_ref[...])
    h = _rmsnorm_rows(x1, nw_ref[...]).astype(BF16)
    for c in range(D_FF // MLP_FF_CHUNK):
        sl = slice(c * MLP_FF_CHUNK, (c + 1) * MLP_FF_CHUNK)
        a = jnp.maximum(_dot(h, w1_ref[:, sl]), 0.0)
        hid_ref[:, sl] = (a * a).astype(BF16)
    o_ref[...] = x1 + _dot(hid_ref[...], w2_ref[...])


def _post_block(x2, ys, wos, norm_w, w1, w2, tm=256):
    m = x2.shape[0]
    n_y = len(ys)
    row = lambda width: pl.BlockSpec((tm, width), lambda i: (i, 0))
    return pl.pallas_call(
        functools.partial(_post_kernel, n_y),
        grid=(m // tm,),
        in_specs=[row(D_MODEL)] + [row(y.shape[1]) for y in ys]
        + [_const_spec(w.shape) for w in wos]
        + [_const_spec((1, D_MODEL)), _const_spec((D_MODEL, D_FF)), _const_spec((D_FF, D_MODEL))],
        out_specs=row(D_MODEL),
        out_shape=jax.ShapeDtypeStruct((m, D_MODEL), F32),
        scratch_shapes=[pltpu.VMEM((tm, D_FF), BF16)],
        compiler_params=pltpu.CompilerParams(
            dimension_semantics=("parallel",), vmem_limit_bytes=VMEM_LIMIT_BYTES),
        name="post_block",
    )(x2, *ys, *wos, norm_w, w1, w2)


def _pad_lanes(vec, offset):
    out = jnp.zeros((1, LANES), F32)
    return out.at[0, offset:offset + vec.shape[0]].set(vec.astype(F32))


def _layer0_params(a_norm_w, a_w_in, ssd_conv_w, ssd_conv_b, ssd_dt_bias, ssd_a_log, ssd_d_skip,
                   ssd_norm_w, gdn_conv_w, gdn_a_log, gdn_dt_bias, gdn_norm_w):
    o_z = 0
    o_xbc = o_z + SSD_INNER
    o_dt = o_xbc + SSD_INNER + 2 * SSD_GROUPS * SSD_STATE
    o_qkv = o_dt + SSD_HEADS
    o_gz = o_qkv + 2 * GDN_QK + GDN_VAL
    o_beta = o_gz + GDN_VAL
    o_alpha = o_beta + GDN_HEADS
    w = a_w_in
    w_main = jnp.concatenate(
        [w[:, o_qkv:o_gz], w[:, o_z:o_xbc], w[:, o_gz:o_beta], w[:, o_xbc:o_dt]], axis=1).astype(BF16)
    w_small = jnp.concatenate(
        [w[:, o_dt:o_qkv], w[:, o_beta:o_alpha], w[:, o_alpha:o_alpha + GDN_HEADS],
         jnp.zeros((D_MODEL, LANES - SSD_HEADS - 2 * GDN_HEADS), w.dtype)], axis=1).astype(BF16)
    head_of_lane = jnp.arange(SSD_INNER) // SSD_HEAD_DIM
    e_mat = (jnp.arange(LANES)[:, None] == head_of_lane[None, :]).astype(BF16)
    head_of_packed = jnp.arange(GDN_HEADS * GDN_CHUNK) // GDN_CHUNK
    src = jnp.arange(LANES)[:, None]
    e64 = ((src == head_of_packed[None, :] + SM_BETA) | (src == head_of_packed[None, :] + SM_ALPHA)).astype(BF16)
    gate_w = GATE_COLS[1] - GATE_COLS[0]
    conv_w = jnp.concatenate([gdn_conv_w.astype(F32), jnp.zeros((CONV_WIDTH, gate_w), F32),
                              ssd_conv_w.astype(F32)], axis=1)
    conv_b = jnp.concatenate([jnp.zeros((GATE_COLS[1],), F32), ssd_conv_b.astype(F32)]).reshape(1, -1)
    conv_w = jnp.repeat(conv_w, SUBLANES, axis=0)
    conv_b = jnp.repeat(conv_b, SUBLANES, axis=0)
    return dict(
        norm_w=a_norm_w.reshape(1, D_MODEL), w_main=w_main, w_small=w_small, conv_w=conv_w, conv_b=conv_b,
        cwx=conv_w[:, RAW_COLS[0]:RAW_COLS[1]], cbx=conv_b[:, RAW_COLS[0]:RAW_COLS[1]],
        ssd_dtb=_pad_lanes(ssd_dt_bias, SM_DT), ssd_alog=_pad_lanes(ssd_a_log, SM_DT),
        dsk=jnp.tile(jnp.repeat(ssd_d_skip.astype(F32), SSD_HEAD_DIM).reshape(1, -1), (SUBLANES, 1)),
        ssd_nw=jnp.tile(ssd_norm_w.astype(F32).reshape(1, -1), (SUBLANES, 1)), e_mat=e_mat,
        gdn_dtb=_pad_lanes(gdn_dt_bias, SM_ALPHA), gdn_alog=_pad_lanes(gdn_a_log, SM_ALPHA),
        gdn_nw=gdn_norm_w.reshape(1, -1), e64=e64)


def kernel(x, a_norm_w, a_w_in, ssd_conv_w, ssd_conv_b, ssd_dt_bias, ssd_a_log, ssd_d_skip, ssd_norm_w,
           gdn_conv_w, gdn_a_log, gdn_dt_bias, gdn_norm_w, a_w_out, c_norm_w, c_w_qkv, c_q_norm_w,
           c_k_norm_w, c_w_o, mlp_norm_w, mlp_w1, mlp_w2):
    bsz, seqlen, d = x.shape
    x2 = x.reshape(bsz * seqlen, d)

    p = _layer0_params(a_norm_w[0], a_w_in[0], ssd_conv_w[0], ssd_conv_b[0], ssd_dt_bias[0], ssd_a_log[0],
                       ssd_d_skip[0], ssd_norm_w[0], gdn_conv_w[0], gdn_a_log[0], gdn_dt_bias[0], gdn_norm_w[0])
    main, small = _in_proj(x2, seqlen, p["norm_w"], p["w_main"], p["w_small"], p["conv_w"], p["conv_b"])
    y_ssd = _ssd_mixer(main, small, bsz, seqlen, p["cwx"], p["cbx"], p["ssd_dtb"], p["ssd_alog"], p["dsk"],
                       p["ssd_nw"], p["e_mat"])
    y_gdn = _gdn_mixer(main, small, bsz, seqlen, p["gdn_dtb"], p["gdn_alog"], p["gdn_nw"], p["e64"])
    w_out = a_w_out[0].astype(BF16)
    x2 = _post_block(x2, [y_ssd, y_gdn], [w_out[:SSD_INNER], w_out[SSD_INNER:]],
                     mlp_norm_w[0].reshape(1, d), mlp_w1[0].astype(BF16), mlp_w2[0].astype(BF16))

    heads_per_chunk = PROJ_COL_CHUNK // SB_HEAD_DIM
    lane_head = jnp.arange(PROJ_COL_CHUNK) // SB_HEAD_DIM
    gmat = (lane_head[:, None] == lane_head[None, :]).astype(BF16)
    qw = jnp.tile(c_q_norm_w[0].astype(F32) * (SB_HEAD_DIM ** -0.5), heads_per_chunk).reshape(1, -1)
    kw = jnp.tile(c_k_norm_w[0].astype(F32), heads_per_chunk).reshape(1, -1)
    qkv = _qkv_proj(x2, c_norm_w[0].reshape(1, d), c_w_qkv[0].astype(BF16), gmat, qw, kw)
    key = jnp.arange(SB_TILE)
    su = (key[:, None] > key[None, :]).astype(BF16)
    o = _sb_attention(qkv, bsz, seqlen, su)
    x2 = _post_block(x2, [o], [c_w_o[0].astype(BF16)],
                     mlp_norm_w[1].reshape(1, d), mlp_w1[1].astype(BF16), mlp_w2[1].astype(BF16))
    return x2.reshape(bsz, seqlen, d)
```

```python
import functools

import jax
import jax.numpy as jnp
from jax import lax
from jax.experimental import pallas as pl
from jax.experimental.pallas import tpu as pltpu

F32 = jnp.float32
BF16 = jnp.bfloat16

EPS = 1e-6
D_MODEL = 1024
D_FF = 4 * D_MODEL
CONV_WIDTH = 4
SSD_HEADS = 16
SSD_HEAD_DIM = 64
SSD_INNER = SSD_HEADS * SSD_HEAD_DIM
SSD_GROUPS = 2
SSD_STATE = 128
SSD_CHUNK = 128
GDN_HEADS = 8
GDN_DK = 128
GDN_DV = 128
GDN_CHUNK = 64
GDN_QK = GDN_HEADS * GDN_DK
GDN_VAL = GDN_HEADS * GDN_DV
SB_HEADS = 16
SB_HEAD_DIM = D_MODEL // SB_HEADS
SB_BLOCK = 128

LANES = 128
SUBLANES = 8
VMEM_LIMIT_BYTES = 56 * 1024 * 1024

SM_DT = 0
SM_BETA = SSD_HEADS
SM_ALPHA = SSD_HEADS + GDN_HEADS
MAIN_WIDTH = 3 * GDN_QK + SSD_INNER + GDN_VAL + SSD_INNER + 2 * SSD_GROUPS * SSD_STATE
COL_Q, COL_K, COL_V, COL_Z, COL_GZ, COL_XS = 0, 1, 2, 3, 4, 5
COL_BC = (3 * GDN_QK + SSD_INNER + GDN_VAL + SSD_INNER) // (2 * SSD_GROUPS * SSD_STATE)

NEG_BIG = -1e30
LOG2_E = 1.4426950408889634


def _dot(a, b):
    return jnp.dot(a, b, preferred_element_type=F32)


def _dot_nt(a, b):
    return lax.dot_general(a, b, (((1,), (1,)), ((), ())), preferred_element_type=F32)


def _dot_tn(a, b):
    return lax.dot_general(a, b, (((0,), (0,)), ((), ())), preferred_element_type=F32)


def _split_hi_lo(v):
    hi = v.astype(BF16)
    lo = (v - hi.astype(F32)).astype(BF16)
    return hi, lo


def _softplus(x):
    return jnp.maximum(x, 0.0) + jnp.log1p(jnp.exp(-jnp.abs(x)))


def _sigmoid(x):
    return 1.0 / (1.0 + jnp.exp(-x))


def _silu(x):
    return x * _sigmoid(x)


def _lower_tri(n, dtype):
    row = lax.broadcasted_iota(jnp.int32, (n, n), 0)
    col = lax.broadcasted_iota(jnp.int32, (n, n), 1)
    return (row >= col).astype(dtype)


def _tile_rows(a8, rows):
    return jnp.concatenate([a8] * (rows // SUBLANES), axis=0)


def _row_to_tile(row, rows):
    return _tile_rows(jnp.broadcast_to(row, (SUBLANES, row.shape[1])), rows)


def _conv_silu(x, tail, w8, b8):
    t, c = x.shape
    tap = lambda k: _tile_rows(w8[k * SUBLANES:(k + 1) * SUBLANES, :], t)
    acc = x * tap(CONV_WIDTH - 1) + _tile_rows(b8, t)
    row8 = lax.broadcasted_iota(jnp.int32, (SUBLANES, c), 0)
    for k in range(1, CONV_WIDTH):
        r = pltpu.roll(x, k, 0)
        tr = pltpu.roll(tail, k, 0)
        first = jnp.where(row8 < k, tr, r[0:SUBLANES])
        shifted = jnp.concatenate([first, r[SUBLANES:]], axis=0)
        acc = acc + shifted * tap(CONV_WIDTH - 1 - k)
    return _silu(acc), x[t - SUBLANES:t]


PROJ_COL_CHUNK = 512
IN_PROJ_COL_CHUNK = 256


def _rmsnorm_rows(x, w):
    ms = jnp.mean(x * x, axis=-1, keepdims=True)
    return x * lax.rsqrt(ms + EPS) * w


GATE_COLS = (COL_Z * GDN_QK, (COL_GZ + 1) * GDN_QK)
RAW_COLS = (COL_XS * SSD_INNER, (COL_XS + 1) * SSD_INNER)


def _in_proj_kernel(tiles_per_seq, x_ref, nw_ref, w_ref, ws_ref, cw_ref, cb_ref, main_ref, small_ref,
                    tail_ref, win_ref):
    tm = x_ref.shape[0]

    @pl.when(pl.program_id(0) % tiles_per_seq == 0)
    def _init():
        tail_ref[...] = jnp.zeros_like(tail_ref)

    h = _rmsnorm_rows(x_ref[...], nw_ref[...]).astype(BF16)
    small_ref[...] = _dot(h, ws_ref[...])
    for c in range(MAIN_WIDTH // IN_PROJ_COL_CHUNK):
        sl = slice(c * IN_PROJ_COL_CHUNK, (c + 1) * IN_PROJ_COL_CHUNK)
        y = _dot(h, w_ref[:, sl])
        if GATE_COLS[0] <= sl.start < GATE_COLS[1]:
            main_ref[:, sl] = _silu(y)
            continue
        if RAW_COLS[0] <= sl.start < RAW_COLS[1]:
            main_ref[:, sl] = y
            continue
        rep = lambda a: jnp.concatenate([a] * (tm // SUBLANES), axis=0)
        w0, w1, w2, w3 = (rep(cw_ref[k * SUBLANES:(k + 1) * SUBLANES, sl]) for k in range(CONV_WIDTH))
        bias = rep(cb_ref[:, sl])
        win_y = win_ref.at[0, c % 2]
        win_u = win_ref.at[1, c % 2]
        win_y[0:SUBLANES, :] = tail_ref[0, :, sl]
        win_y[SUBLANES:, :] = y
        tail_ref[0, :, sl] = y[tm - SUBLANES:tm]
        y1 = win_y[pl.ds(SUBLANES - 1, tm), :]
        u = y * w1 + y1 * w0
        win_u[0:SUBLANES, :] = tail_ref[1, :, sl]
        win_u[SUBLANES:, :] = u
        tail_ref[1, :, sl] = u[tm - SUBLANES:tm]
        acc = y * w3 + y1 * w2 + win_u[pl.ds(SUBLANES - 2, tm), :] + bias
        main_ref[:, sl] = _silu(acc)


def _const_spec(shape):
    nd = len(shape)
    return pl.BlockSpec(shape, lambda *_: (0,) * nd, pipeline_mode=pl.Buffered(1))


def _in_proj(x2, seqlen, norm_w, w_main, w_small, conv_w, conv_b, tm=256):
    m = x2.shape[0]
    return pl.pallas_call(
        functools.partial(_in_proj_kernel, seqlen // tm),
        grid=(m // tm,),
        in_specs=[
            pl.BlockSpec((tm, D_MODEL), lambda i: (i, 0)),
            _const_spec((1, D_MODEL)),
            _const_spec((D_MODEL, MAIN_WIDTH)),
            _const_spec((D_MODEL, LANES)),
            _const_spec((CONV_WIDTH * SUBLANES, MAIN_WIDTH)),
            _const_spec((SUBLANES, MAIN_WIDTH)),
        ],
        out_specs=[
            pl.BlockSpec((tm, MAIN_WIDTH), lambda i: (i, 0)),
            pl.BlockSpec((tm, LANES), lambda i: (i, 0)),
        ],
        out_shape=[
            jax.ShapeDtypeStruct((m, MAIN_WIDTH), F32),
            jax.ShapeDtypeStruct((m, LANES), F32),
        ],
        scratch_shapes=[pltpu.VMEM((2, SUBLANES, MAIN_WIDTH), F32),
                        pltpu.VMEM((2, 2, SUBLANES + tm, IN_PROJ_COL_CHUNK), F32)],
        compiler_params=pltpu.CompilerParams(
            dimension_semantics=("arbitrary",), vmem_limit_bytes=VMEM_LIMIT_BYTES),
        name="in_proj",
    )(x2, norm_w, w_main, w_small, conv_w, conv_b)


def _qkv_proj_kernel(x_ref, nw_ref, w_ref, gmat_ref, qw_ref, kw_ref, o_ref):
    h = _rmsnorm_rows(x_ref[...], nw_ref[...]).astype(BF16)
    gmat = gmat_ref[...]
    n_chunks = D_MODEL // PROJ_COL_CHUNK
    for part, hw_ref in ((0, qw_ref), (1, kw_ref)):
        for c in range(n_chunks):
            col = part * D_MODEL + c * PROJ_COL_CHUNK
            y = _dot(h, w_ref[:, col:col + PROJ_COL_CHUNK])
            ms = _dot((y * y).astype(BF16), gmat) * (1.0 / SB_HEAD_DIM)
            o_ref[:, col:col + PROJ_COL_CHUNK] = (y * lax.rsqrt(ms + EPS) * hw_ref[...]).astype(o_ref.dtype)
    for c in range(n_chunks):
        col = 2 * D_MODEL + c * PROJ_COL_CHUNK
        o_ref[:, col:col + PROJ_COL_CHUNK] = _dot(h, w_ref[:, col:col + PROJ_COL_CHUNK]).astype(o_ref.dtype)


def _qkv_proj(x2, norm_w, w_qkv, gmat, qw, kw, tm=512):
    m = x2.shape[0]
    return pl.pallas_call(
        _qkv_proj_kernel,
        grid=(m // tm,),
        in_specs=[
            pl.BlockSpec((tm, D_MODEL), lambda i: (i, 0)),
            _const_spec((1, D_MODEL)),
            _const_spec((D_MODEL, 3 * D_MODEL)),
            _const_spec((PROJ_COL_CHUNK, PROJ_COL_CHUNK)),
            _const_spec((1, PROJ_COL_CHUNK)),
            _const_spec((1, PROJ_COL_CHUNK)),
        ],
        out_specs=pl.BlockSpec((tm, 3 * D_MODEL), lambda i: (i, 0)),
        out_shape=jax.ShapeDtypeStruct((m, 3 * D_MODEL), BF16),
        compiler_params=pltpu.CompilerParams(
            dimension_semantics=("parallel",), vmem_limit_bytes=VMEM_LIMIT_BYTES),
        name="qkv_proj",
    )(x2, norm_w, w_qkv, gmat, qw, kw)


SSD_STEP_CHUNKS = 4


def _ssd_kernel(gate_ref, xs_ref, bc_ref, sm_ref, cwx_ref, cbx_ref, dtb_ref, alog_ref, dsk_ref, nw_ref, e_ref,
                y_ref, tailx_ref, state_ref):
    t = SSD_CHUNK
    n = SSD_STATE
    hpg = SSD_HEADS // SSD_GROUPS
    gw = hpg * SSD_HEAD_DIM
    chunks = range(SSD_STEP_CHUNKS)
    rows = [slice(c * t, (c + 1) * t) for c in chunks]
    groups = range(SSD_GROUPS)

    @pl.when(pl.program_id(1) == 0)
    def _init():
        tailx_ref[...] = jnp.zeros_like(tailx_ref)
        state_ref[...] = jnp.zeros_like(state_ref)

    xs_all, tailx_ref[...] = _conv_silu(xs_ref[...], tailx_ref[...], cwx_ref[...], cbx_ref[...])
    bc_b = bc_ref[...].astype(BF16)
    gate = gate_ref[...]
    dsk = _tile_rows(dsk_ref[...], t)
    nw = _tile_rows(nw_ref[...], t)

    lane = lax.broadcasted_iota(jnp.int32, (1, LANES), 1)
    head_lane = (lane >= SM_DT) & (lane < SM_DT + SSD_HEADS)
    dt_all = jnp.where(head_lane, _softplus(sm_ref[...] + dtb_ref[...]), 0.0)
    a_all = dt_all * (-jnp.exp(alog_ref[...]))
    tri = _lower_tri(t, BF16)
    a_cum, dt = [], []
    for c in chunks:
        a_hi, a_lo = _split_hi_lo(a_all[rows[c]])
        a_cum.append(_dot(tri, a_hi) + _dot(tri, a_lo))
        dt.append(dt_all[rows[c]])
    a_cum_t = [a_cum[c].T for c in chunks]
    dt_t = [dt[c].T for c in chunks]
    e_a = [jnp.exp(a_cum[c]) for c in chunks]
    w_state = [dt[c] * jnp.exp(a_cum[c][t - 1:t, :] - a_cum[c]) for c in chunks]

    e_mat = e_ref[...]
    ex = []
    for c in chunks:
        ex_hi, ex_lo = _split_hi_lo(jnp.concatenate([w_state[c], e_a[c]], axis=0))
        ex.append(_dot(ex_hi, e_mat) + _dot(ex_lo, e_mat))
    w_state_x = [ex[c][:t] for c in chunks]
    e_a_x = [ex[c][t:] for c in chunks]

    xs = [xs_all[rows[c]] for c in chunks]
    xw = [(xs[c] * w_state_x[c]).astype(BF16) for c in chunks]
    b_g = [[bc_b[rows[c], g * n:(g + 1) * n] for g in groups] for c in chunks]
    c_g = [[bc_b[rows[c], (SSD_GROUPS + g) * n:(SSD_GROUPS + g + 1) * n] for g in groups] for c in chunks]
    cb = [[_dot_nt(c_g[c][g], b_g[c][g]) for g in groups] for c in chunks]
    upd = [jnp.concatenate([_dot_tn(b_g[c][g], xw[c][:, g * gw:(g + 1) * gw]) for g in groups], axis=1)
           for c in chunks]

    state = state_ref[...]
    y_off = []
    for c in chunks:
        state_b = state.astype(BF16)
        y_off.append(jnp.concatenate([_dot(c_g[c][g], state_b[:, g * gw:(g + 1) * gw]) for g in groups], axis=1)
                     * e_a_x[c])
        state = state * _row_to_tile(e_a_x[c][t - 1:t, :], n) + upd[c]
    state_ref[...] = state

    row = lax.broadcasted_iota(jnp.int32, (t, t), 0)
    col = lax.broadcasted_iota(jnp.int32, (t, t), 1)
    causal = row >= col
    xs_b = [xs[c].astype(BF16) for c in chunks]
    y_heads = [[None] * SSD_HEADS for _ in chunks]
    for h in range(SSD_HEADS):
        g = h // hpg
        for c in chunks:
            a_col = jnp.broadcast_to(a_cum[c][:, h:h + 1], (t, t))
            seg = jnp.where(causal, a_col - _row_to_tile(a_cum_t[c][h:h + 1, :], t), NEG_BIG)
            m = (cb[c][g] * jnp.exp(seg) * _row_to_tile(dt_t[c][h:h + 1, :], t)).astype(BF16)
            y_heads[c][h] = _dot(m, xs_b[c][:, h * SSD_HEAD_DIM:(h + 1) * SSD_HEAD_DIM])

    outs = []
    for c in chunks:
        y = jnp.concatenate(y_heads[c], axis=1) + y_off[c] + xs[c] * dsk
        y = y * gate[rows[c]]
        parts = []
        for g in groups:
            yg = y[:, g * gw:(g + 1) * gw]
            ms = jnp.mean(yg * yg, axis=-1, keepdims=True)
            parts.append(yg * lax.rsqrt(ms + EPS))
        outs.append((jnp.concatenate(parts, axis=1) * nw).astype(y_ref.dtype))
    y_ref[...] = jnp.concatenate(outs, axis=0)


def _ssd_mixer(main, small, bsz, seqlen, cwx, cbx, dtb, alog, dsk, nw, e_mat):
    t = SSD_STEP_CHUNKS * SSD_CHUNK
    nc = seqlen // t
    bc_w = 2 * SSD_GROUPS * SSD_STATE
    return pl.pallas_call(
        _ssd_kernel,
        grid=(bsz, nc),
        in_specs=[
            pl.BlockSpec((t, SSD_INNER), lambda b, c: (b * nc + c, COL_Z)),
            pl.BlockSpec((t, SSD_INNER), lambda b, c: (b * nc + c, COL_XS)),
            pl.BlockSpec((t, bc_w), lambda b, c: (b * nc + c, COL_BC)),
            pl.BlockSpec((t, LANES), lambda b, c: (b * nc + c, 0)),
            _const_spec((CONV_WIDTH * SUBLANES, SSD_INNER)),
            _const_spec((SUBLANES, SSD_INNER)),
            _const_spec((1, LANES)),
            _const_spec((1, LANES)),
            _const_spec((SUBLANES, SSD_INNER)),
            _const_spec((SUBLANES, SSD_INNER)),
            _const_spec((LANES, SSD_INNER)),
        ],
        out_specs=pl.BlockSpec((t, SSD_INNER), lambda b, c: (b * nc + c, 0)),
        out_shape=jax.ShapeDtypeStruct((bsz * seqlen, SSD_INNER), BF16),
        scratch_shapes=[pltpu.VMEM((SUBLANES, SSD_INNER), F32), pltpu.VMEM((SSD_STATE, SSD_INNER), F32)],
        compiler_params=pltpu.CompilerParams(
            dimension_semantics=("parallel", "arbitrary"), vmem_limit_bytes=VMEM_LIMIT_BYTES),
        name="ssd_mixer",
    )(main, main, main, small, cwx, cbx, dtb, alog, dsk, nw, e_mat)


def _l2norm_rows(x):
    return x * lax.rsqrt(jnp.sum(x * x, axis=-1, keepdims=True) + EPS)


GDN_STEP_CHUNKS = 8
GDN_PAIR = 2
GDN_QUAD = 4


def _pair_block_diag(m):
    lane = lax.broadcasted_iota(jnp.int32, m.shape, 1)
    zero = jnp.zeros((), m.dtype)
    return jnp.concatenate([jnp.where(lane < LANES, m, zero), jnp.where(lane >= LANES, m, zero)], axis=0)


def _packed_rows(rows8):
    n_pairs = GDN_HEADS // 2
    head = lax.broadcasted_iota(jnp.int32, rows8.shape, 0)
    m = jnp.where((head & 1) == 1, pltpu.roll(rows8, GDN_CHUNK, 1), rows8)
    m4 = jnp.concatenate([m] * n_pairs, axis=1)
    head4 = lax.broadcasted_iota(jnp.int32, m4.shape, 0)
    pair_of_lane = lax.broadcasted_iota(jnp.int32, m4.shape, 1) // LANES
    y = jnp.where((head4 >> 1) == pair_of_lane, m4, 0.0)
    for shift in (4, 2, 1):
        y = y + pltpu.roll(y, shift, 0)
    return jnp.concatenate([y] * (GDN_CHUNK // SUBLANES), axis=0)


def _gdn_kernel(q_ref, k_ref, v_ref, gate_ref, sm_ref, dtb_ref, alog_ref, nw_ref, e64_ref,
                o_ref, s_ref):
    t = GDN_CHUNK

    @pl.when(pl.program_id(1) == 0)
    def _init():
        s_ref[...] = jnp.zeros_like(s_ref)

    q = q_ref[...]
    k = k_ref[...]
    v_b = v_ref[...].astype(BF16)
    sm = sm_ref[...]
    gate = gate_ref[...]
    nw = nw_ref[...]
    e64 = e64_ref[...]
    nc = GDN_STEP_CHUNKS
    chunks = range(nc)
    pairs = range(GDN_HEADS // GDN_PAIR)
    quads = range(GDN_HEADS // GDN_QUAD)
    rows = [slice(c * t, (c + 1) * t) for c in chunks]
    pw = GDN_HEADS * t
    pair_w = GDN_PAIR * GDN_DK
    pair_pw = GDN_PAIR * t
    psl = [slice(p * pair_w, (p + 1) * pair_w) for p in pairs]
    ppl = [slice(p * pair_pw, (p + 1) * pair_pw) for p in pairs]

    qn, kn = [], []
    for h in range(GDN_HEADS):
        sl = slice(h * GDN_DK, (h + 1) * GDN_DK)
        qn.append(_l2norm_rows(q[:, sl]) * (GDN_DK ** -0.5))
        kn.append(_l2norm_rows(k[:, sl]))
    q_b = jnp.concatenate(qn, axis=1).astype(BF16)
    k_b = jnp.concatenate(kn, axis=1).astype(BF16)

    lane = lax.broadcasted_iota(jnp.int32, (1, LANES), 1)
    alpha_lane = (lane >= SM_ALPHA) & (lane < SM_ALPHA + GDN_HEADS)
    beta_lane = (lane >= SM_BETA) & (lane < SM_BETA + GDN_HEADS)
    beta = jnp.where(beta_lane, _sigmoid(sm), 0.0)
    g = jnp.where(alpha_lane, -jnp.exp(alog_ref[...]) * _softplus(sm + dtb_ref[...]), 0.0)
    row_s = lax.broadcasted_iota(jnp.int32, (nc * t, nc * t), 0)
    col_s = lax.broadcasted_iota(jnp.int32, (nc * t, nc * t), 1)
    tri = ((row_s >= col_s) & ((row_s // t) == (col_s // t))).astype(BF16)
    g_hi, g_lo = _split_hi_lo(g)
    gc_all = _dot(tri, g_hi) + _dot(tri, g_lo)
    e_gc_all = jnp.exp(gc_all)
    gc = [gc_all[r] for r in rows]
    e_gc = [e_gc_all[r] for r in rows]
    e_k = [jnp.exp(gc[c][t - 1:t, :] - gc[c]) for c in chunks]

    pad = jnp.zeros((LANES - t, LANES), F32)
    gc_rows8 = [jnp.concatenate([gc[c], pad], axis=0).T[SM_ALPHA:SM_ALPHA + GDN_HEADS, :] for c in chunks]
    beta_rows8 = [jnp.concatenate([beta[rows[c]], pad], axis=0).T[SM_BETA:SM_BETA + GDN_HEADS, :] for c in chunks]
    gc_rp = [_packed_rows(gc_rows8[c]) for c in chunks]
    beta_rp = [_packed_rows(beta_rows8[c]) for c in chunks]
    be_rp = [_packed_rows(beta_rows8[c] * jnp.exp(gc_rows8[c])) for c in chunks]
    gc_hi, gc_lo = _split_hi_lo(gc_all)
    ex = _dot(jnp.concatenate([gc_hi, gc_lo, beta.astype(BF16)], axis=0), e64)
    gc_cp_all = ex[:nc * t] + ex[nc * t:2 * nc * t]
    beta_cp_all = ex[2 * nc * t:]

    rowp = lax.broadcasted_iota(jnp.int32, (t, pw), 0)
    colp = lax.broadcasted_iota(jnp.int32, (t, pw), 1) & (t - 1)
    causal_p = colp <= rowp
    strict_p = colp < rowp
    decay_p = [jnp.exp(jnp.where(causal_p, gc_cp_all[rows[c]] - gc_rp[c], NEG_BIG)) for c in chunks]

    kq = [[_dot_nt(jnp.concatenate([k_b[rows[c], psl[p]], q_b[rows[c], psl[p]]], axis=0),
                   _pair_block_diag(k_b[rows[c], psl[p]])) for p in pairs] for c in chunks]
    amat = [jnp.where(strict_p, jnp.concatenate([kq[c][p][:t] for p in pairs], axis=1)
                      * beta_cp_all[rows[c]] * decay_p[c], 0.0) for c in chunks]
    qk_b = [jnp.where(causal_p, jnp.concatenate([kq[c][p][t:] for p in pairs], axis=1) * decay_p[c], 0.0).astype(BF16)
            for c in chunks]

    qd = GDN_QUAD * t
    blk_r = lax.broadcasted_iota(jnp.int32, (qd, qd), 0) // t
    blk_c = lax.broadcasted_iota(jnp.int32, (qd, qd), 1) // t
    same_block = blk_r == blk_c
    eye_q = ((lax.broadcasted_iota(jnp.int32, (t, qd), 1) & (t - 1))
             == lax.broadcasted_iota(jnp.int32, (t, qd), 0)).astype(F32)

    def block_diag4(m):
        return jnp.where(same_block, jnp.concatenate([m] * GDN_QUAD, axis=0), 0.0).astype(BF16)

    units = [(c, d) for c in chunks for d in quads]
    a_q = {u: amat[u[0]][:, u[1] * qd:(u[1] + 1) * qd] for u in units}
    x_q = {u: eye_q - a_q[u] for u in units}
    p_q = {u: _dot(a_q[u].astype(BF16), block_diag4(a_q[u])) for u in units}
    n_sq = (t - 1).bit_length() - 1
    for it in range(n_sq):
        last = it + 1 == n_sq
        xp = {u: _dot((x_q[u] if last else jnp.concatenate([x_q[u], p_q[u]], axis=0)).astype(BF16),
                      block_diag4(p_q[u])) for u in units}
        x_q = {u: x_q[u] + xp[u][:t] for u in units}
        if not last:
            p_q = {u: xp[u][t:] for u in units}
    t_inv = [jnp.concatenate([x_q[(c, d)] for d in quads], axis=1) for c in chunks]
    tb = [(t_inv[c] * beta_rp[c]).astype(BF16) for c in chunks]
    tbe = [(t_inv[c] * be_rp[c]).astype(BF16) for c in chunks]

    u_base = [jnp.concatenate([_dot(tb[c][:, ppl[p]], _pair_block_diag(v_b[rows[c], psl[p]])) for p in pairs], axis=1)
              for c in chunks]
    w_b = [[_dot(tbe[c][:, ppl[p]], _pair_block_diag(k_b[rows[c], psl[p]])).astype(BF16) for p in pairs]
           for c in chunks]

    def expand_cols(x, base):
        return jnp.concatenate(
            [jnp.broadcast_to(x[:, base + h:base + h + 1], (t, GDN_DV)) for h in range(GDN_HEADS)], axis=1)

    e_gc_x = [expand_cols(e_gc[c], SM_ALPHA) for c in chunks]
    e_k_x = [expand_cols(e_k[c], SM_ALPHA) for c in chunks]

    s_all = s_ref[...]
    o_chunks = []
    for c in chunks:
        s_b = s_all.astype(BF16)
        r = [_dot(jnp.concatenate([w_b[c][p], q_b[rows[c], psl[p]]], axis=0), _pair_block_diag(s_b[:, psl[p]]))
             for p in pairs]
        u = u_base[c] - jnp.concatenate([r[p][:t] for p in pairs], axis=1)
        u_b = u.astype(BF16)
        u_dec_b = (u * e_k_x[c]).astype(BF16)
        o2 = [_dot(qk_b[c][:, ppl[p]], _pair_block_diag(u_b[:, psl[p]])) for p in pairs]
        full = [_dot_tn(k_b[rows[c], psl[p]], u_dec_b[:, psl[p]]) for p in pairs]
        upd = jnp.concatenate([blk for p in pairs for blk in (full[p][:GDN_DK, :GDN_DV], full[p][GDN_DK:, GDN_DV:])],
                              axis=1)
        s_all = s_all * e_gc_x[c][t - 1:t, :] + upd
        o_chunks.append(jnp.concatenate([r[p][t:] for p in pairs], axis=1) * e_gc_x[c] + jnp.concatenate(o2, axis=1))
    s_ref[...] = s_all

    o = jnp.concatenate(o_chunks, axis=0)
    outs = []
    for h in range(GDN_HEADS):
        sl = slice(h * GDN_DV, (h + 1) * GDN_DV)
        oh = o[:, sl]
        oh = oh * lax.rsqrt(jnp.mean(oh * oh, axis=-1, keepdims=True) + EPS) * nw
        outs.append((oh * gate[:, sl]).astype(o_ref.dtype))
    o_ref[...] = jnp.concatenate(outs, axis=1)


def _gdn_mixer(main, small, bsz, seqlen, dtb, alog, nw, e64):
    t = GDN_STEP_CHUNKS * GDN_CHUNK
    nc = seqlen // t
    row_spec = lambda colblk: pl.BlockSpec((t, GDN_QK), lambda b, c: (b * nc + c, colblk))
    return pl.pallas_call(
        _gdn_kernel,
        grid=(bsz, nc),
        in_specs=[
            row_spec(COL_Q), row_spec(COL_K), row_spec(COL_V), row_spec(COL_GZ),
            pl.BlockSpec((t, LANES), lambda b, c: (b * nc + c, 0)),
            _const_spec((1, LANES)),
            _const_spec((1, LANES)),
            _const_spec((1, GDN_DV)),
            _const_spec((LANES, GDN_HEADS * GDN_CHUNK)),
        ],
        out_specs=pl.BlockSpec((t, GDN_VAL), lambda b, c: (b * nc + c, 0)),
        out_shape=jax.ShapeDtypeStruct((bsz * seqlen, GDN_VAL), BF16),
        scratch_shapes=[pltpu.VMEM((GDN_DK, GDN_HEADS * GDN_DV), F32)],
        compiler_params=pltpu.CompilerParams(
            dimension_semantics=("parallel", "arbitrary"), vmem_limit_bytes=VMEM_LIMIT_BYTES),
        name="gdn_mixer",
    )(main, main, main, main, small, dtb, alog, nw, e64)


SB_TILE = 256
SB_LANE_TILES = 8
SB_HEADS_PER_LANE_TILE = LANES // SB_HEAD_DIM


def _sb_kernel(q_ref, k_ref, v_ref, su_ref, o_ref, acc_ref, carry_ref):
    t = SB_TILE
    qi = pl.program_id(2)
    lane = lax.broadcasted_iota(jnp.int32, (1, LANES), 1)
    head_masks = [(lane // SB_HEAD_DIM) == p for p in range(SB_HEADS_PER_LANE_TILE)]
    zero_b = jnp.zeros((), BF16)
    q = q_ref[...]
    qm = [[jnp.where(hm, q[:, g * LANES:(g + 1) * LANES], zero_b) for hm in head_masks]
          for g in range(SB_LANE_TILES)]
    row = lax.broadcasted_iota(jnp.int32, (t, t), 0)
    col = lax.broadcasted_iota(jnp.int32, (t, t), 1)
    valid = col < row
    su = su_ref[...]

    acc_ref[...] = jnp.zeros_like(acc_ref)
    carry_ref[...] = jnp.zeros_like(carry_ref)

    def key_tile(j, diagonal):
        start = pl.multiple_of(j * t, t)
        k_all = k_ref[pl.ds(start, t), :]
        v_all = v_ref[pl.ds(start, t), :]
        carry_all = carry_ref[...]
        heads = [(g, p) for g in range(SB_LANE_TILES) for p in range(SB_HEADS_PER_LANE_TILE)]
        lanes_of = lambda g: slice(g * LANES, (g + 1) * LANES)
        s = [_dot_nt(qm[g][p], k_all[:, lanes_of(g)]) for g, p in heads]
        log_beta, log_1m = [], []
        for i in range(len(heads)):
            sb = s[i].astype(BF16)
            soft = jnp.log(1.0 + jnp.exp2(jnp.abs(sb) * (-LOG2_E)))
            lb = jnp.minimum(sb, 0.0) - soft
            lm = lb - sb
            if diagonal:
                lm = jnp.where(valid, lm, jnp.zeros((), BF16))
            log_beta.append(lb)
            log_1m.append(lm)
        suffix = [_dot(lm, su) for lm in log_1m]
        parts, carries = [], []
        for i, (g, p) in enumerate(heads):
            carry = carry_all[:, i * LANES:(i + 1) * LANES]
            att = jnp.exp(log_beta[i].astype(F32) + suffix[i] + jnp.concatenate([carry] * (t // LANES), axis=1))
            if diagonal:
                att = jnp.where(valid, att, 0.0)
            parts.append(_dot(att.astype(BF16), jnp.where(head_masks[p], v_all[:, lanes_of(g)], zero_b)))
            carries.append(carry + jnp.sum(log_1m[i].astype(F32), axis=-1, keepdims=True))
        outs = [functools.reduce(lambda a, b: a + b, parts[g * SB_HEADS_PER_LANE_TILE:(g + 1) * SB_HEADS_PER_LANE_TILE])
                for g in range(SB_LANE_TILES)]
        acc_ref[...] += jnp.concatenate(outs, axis=1)
        carry_ref[...] = jnp.concatenate(carries, axis=1)

    key_tile(qi, True)

    def body(i, c):
        key_tile(qi - 1 - i, False)
        return c

    lax.fori_loop(0, qi, body, 0)
    o_ref[...] = acc_ref[...].astype(o_ref.dtype)


def _sb_attention(qkv, bsz, seqlen, su):
    t = SB_TILE
    nq = seqlen // t
    w = SB_LANE_TILES * LANES
    n_groups = D_MODEL // w
    return pl.pallas_call(
        _sb_kernel,
        grid=(bsz, n_groups, nq),
        in_specs=[
            pl.BlockSpec((t, w), lambda b, p, i: (b * nq + i, p)),
            pl.BlockSpec((seqlen, w), lambda b, p, i: (b, n_groups + p)),
            pl.BlockSpec((seqlen, w), lambda b, p, i: (b, 2 * n_groups + p)),
            _const_spec((t, t)),
        ],
        out_specs=pl.BlockSpec((t, w), lambda b, p, i: (b * nq + i, p)),
        out_shape=jax.ShapeDtypeStruct((bsz * seqlen, D_MODEL), BF16),
        scratch_shapes=[
            pltpu.VMEM((t, w), F32),
            pltpu.VMEM((t, SB_LANE_TILES * SB_HEADS_PER_LANE_TILE * LANES), F32),
        ],
        compiler_params=pltpu.CompilerParams(
            dimension_semantics=("parallel", "parallel", "arbitrary"), vmem_limit_bytes=VMEM_LIMIT_BYTES),
        name="sb_attention",
    )(qkv, qkv, qkv, su)


MLP_FF_CHUNK = 512


def _post_kernel(n_y, x_ref, *refs):
    y_refs = refs[:n_y]
    wo_refs = refs[n_y:2 * n_y]
    nw_ref, w1_ref, w2_ref, o_ref, hid_ref = refs[2 * n_y:]
    x1 = x_ref[...]
    for y_ref, wo_ref in zip(y_refs, wo_refs):
        x1 = x1 + _dot(y_ref[...], wo_ref[...])
    h = _rmsnorm_rows(x1, nw_ref[...]).astype(BF16)
    for c in range(D_FF // MLP_FF_CHUNK):
        sl = slice(c * MLP_FF_CHUNK, (c + 1) * MLP_FF_CHUNK)
        a = jnp.maximum(_dot(h, w1_ref[:, sl]), 0.0)
        hid_ref[:, sl] = (a * a).astype(BF16)
    o_ref[...] = x1 + _dot(hid_ref[...], w2_ref[...])


def _post_block(x2, ys, wos, norm_w, w1, w2, tm=512):
    m = x2.shape[0]
    n_y = len(ys)
    row = lambda width: pl.BlockSpec((tm, width), lambda i: (i, 0))
    return pl.pallas_call(
        functools.partial(_post_kernel, n_y),
        grid=(m // tm,),
        in_specs=[row(D_MODEL)] + [row(y.shape[1]) for y in ys]
        + [_const_spec(w.shape) for w in wos]
        + [_const_spec((1, D_MODEL)), _const_spec((D_MODEL, D_FF)), _const_spec((D_FF, D_MODEL))],
        out_specs=row(D_MODEL),
        out_shape=jax.ShapeDtypeStruct((m, D_MODEL), F32),
        scratch_shapes=[pltpu.VMEM((tm, D_FF), BF16)],
        compiler_params=pltpu.CompilerParams(
            dimension_semantics=("parallel",), vmem_limit_bytes=VMEM_LIMIT_BYTES),
        name="post_block",
    )(x2, *ys, *wos, norm_w, w1, w2)


def _pad_lanes(vec, offset):
    out = jnp.zeros((1, LANES), F32)
    return out.at[0, offset:offset + vec.shape[0]].set(vec.astype(F32))


def _layer0_params(a_norm_w, a_w_in, ssd_conv_w, ssd_conv_b, ssd_dt_bias, ssd_a_log, ssd_d_skip,
                   ssd_norm_w, gdn_conv_w, gdn_a_log, gdn_dt_bias, gdn_norm_w):
    o_z = 0
    o_xbc = o_z + SSD_INNER
    o_dt = o_xbc + SSD_INNER + 2 * SSD_GROUPS * SSD_STATE
    o_qkv = o_dt + SSD_HEADS
    o_gz = o_qkv + 2 * GDN_QK + GDN_VAL
    o_beta = o_gz + GDN_VAL
    o_alpha = o_beta + GDN_HEADS
    w = a_w_in
    w_main = jnp.concatenate(
        [w[:, o_qkv:o_gz], w[:, o_z:o_xbc], w[:, o_gz:o_beta], w[:, o_xbc:o_dt]], axis=1).astype(BF16)
    w_small = jnp.concatenate(
        [w[:, o_dt:o_qkv], w[:, o_beta:o_alpha], w[:, o_alpha:o_alpha + GDN_HEADS],
         jnp.zeros((D_MODEL, LANES - SSD_HEADS - 2 * GDN_HEADS), w.dtype)], axis=1).astype(BF16)
    head_of_lane = jnp.arange(SSD_INNER) // SSD_HEAD_DIM
    e_mat = (jnp.arange(LANES)[:, None] == head_of_lane[None, :]).astype(BF16)
    head_of_packed = jnp.arange(GDN_HEADS * GDN_CHUNK) // GDN_CHUNK
    src = jnp.arange(LANES)[:, None]
    e64 = ((src == head_of_packed[None, :] + SM_BETA) | (src == head_of_packed[None, :] + SM_ALPHA)).astype(BF16)
    gate_w = GATE_COLS[1] - GATE_COLS[0]
    conv_w = jnp.concatenate([gdn_conv_w.astype(F32), jnp.zeros((CONV_WIDTH, gate_w), F32),
                              ssd_conv_w.astype(F32)], axis=1)
    conv_b = jnp.concatenate([jnp.zeros((GATE_COLS[1],), F32), ssd_conv_b.astype(F32)]).reshape(1, -1)
    conv_w = jnp.repeat(conv_w, SUBLANES, axis=0)
    conv_b = jnp.repeat(conv_b, SUBLANES, axis=0)
    return dict(
        norm_w=a_norm_w.reshape(1, D_MODEL), w_main=w_main, w_small=w_small, conv_w=conv_w, conv_b=conv_b,
        cwx=conv_w[:, RAW_COLS[0]:RAW_COLS[1]], cbx=conv_b[:, RAW_COLS[0]:RAW_COLS[1]],
        ssd_dtb=_pad_lanes(ssd_dt_bias, SM_DT), ssd_alog=_pad_lanes(ssd_a_log, SM_DT),
        dsk=jnp.tile(jnp.repeat(ssd_d_skip.astype(F32), SSD_HEAD_DIM).reshape(1, -1), (SUBLANES, 1)),
        ssd_nw=jnp.tile(ssd_norm_w.astype(F32).reshape(1, -1), (SUBLANES, 1)), e_mat=e_mat,
        gdn_dtb=_pad_lanes(gdn_dt_bias, SM_ALPHA), gdn_alog=_pad_lanes(gdn_a_log, SM_ALPHA),
        gdn_nw=gdn_norm_w.reshape(1, -1), e64=e64)


def kernel(x, a_norm_w, a_w_in, ssd_conv_w, ssd_conv_b, ssd_dt_bias, ssd_a_log, ssd_d_skip, ssd_norm_w,
           gdn_conv_w, gdn_a_log, gdn_dt_bias, gdn_norm_w, a_w_out, c_norm_w, c_w_qkv, c_q_norm_w,
           c_k_norm_w, c_w_o, mlp_norm_w, mlp_w1, mlp_w2):
    bsz, seqlen, d = x.shape
    x2 = x.reshape(bsz * seqlen, d)

    p = _layer0_params(a_norm_w[0], a_w_in[0], ssd_conv_w[0], ssd_conv_b[0], ssd_dt_bias[0], ssd_a_log[0],
                       ssd_d_skip[0], ssd_norm_w[0], gdn_conv_w[0], gdn_a_log[0], gdn_dt_bias[0], gdn_norm_w[0])
    main, small = _in_proj(x2, seqlen, p["norm_w"], p["w_main"], p["w_small"], p["conv_w"], p["conv_b"])
    y_ssd = _ssd_mixer(main, small, bsz, seqlen, p["cwx"], p["cbx"], p["ssd_dtb"], p["ssd_alog"], p["dsk"],
                       p["ssd_nw"], p["e_mat"])
    y_gdn = _gdn_mixer(main, small, bsz, seqlen, p["gdn_dtb"], p["gdn_alog"], p["gdn_nw"], p["e64"])
    w_out = a_w_out[0].astype(BF16)
    x2 = _post_block(x2, [y_ssd, y_gdn], [w_out[:SSD_INNER], w_out[SSD_INNER:]],
                     mlp_norm_w[0].reshape(1, d), mlp_w1[0].astype(BF16), mlp_w2[0].astype(BF16))

    heads_per_chunk = PROJ_COL_CHUNK // SB_HEAD_DIM
    lane_head = jnp.arange(PROJ_COL_CHUNK) // SB_HEAD_DIM
    gmat = (lane_head[:, None] == lane_head[None, :]).astype(BF16)
    qw = jnp.tile(c_q_norm_w[0].astype(F32) * (SB_HEAD_DIM ** -0.5), heads_per_chunk).reshape(1, -1)
    kw = jnp.tile(c_k_norm_w[0].astype(F32), heads_per_chunk).reshape(1, -1)
    qkv = _qkv_proj(x2, c_norm_w[0].reshape(1, d), c_w_qkv[0].astype(BF16), gmat, qw, kw)
    key = jnp.arange(SB_TILE)
    su = (key[:, None] > key[None, :]).astype(BF16)
    o = _sb_attention(qkv, bsz, seqlen, su)
    x2 = _post_block(x2, [o], [c_w_o[0].astype(BF16)],
                     mlp_norm_w[1].reshape(1, d), mlp_w1[1].astype(BF16), mlp_w2[1].astype(BF16))
    return x2.reshape(bsz, seqlen, d)
```

```python
import functools

import jax
import jax.numpy as jnp
from jax import lax
from jax.experimental import pallas as pl
from jax.experimental.pallas import tpu as pltpu

F32 = jnp.float32
BF16 = jnp.bfloat16

EPS = 1e-6
D_MODEL = 1024
D_FF = 4 * D_MODEL
CONV_WIDTH = 4
SSD_HEADS = 16
SSD_HEAD_DIM = 64
SSD_INNER = SSD_HEADS * SSD_HEAD_DIM
SSD_GROUPS = 2
SSD_STATE = 128
SSD_CHUNK = 128
GDN_HEADS = 8
GDN_DK = 128
GDN_DV = 128
GDN_CHUNK = 64
GDN_QK = GDN_HEADS * GDN_DK
GDN_VAL = GDN_HEADS * GDN_DV
SB_HEADS = 16
SB_HEAD_DIM = D_MODEL // SB_HEADS

LANES = 128
SUBLANES = 8
VMEM_LIMIT_BYTES = 56 * 1024 * 1024

SM_DT = 0
SM_BETA = SSD_HEADS
SM_ALPHA = SSD_HEADS + GDN_HEADS
MAIN_WIDTH = 3 * GDN_QK + SSD_INNER + GDN_VAL + SSD_INNER + 2 * SSD_GROUPS * SSD_STATE
COL_Q, COL_K, COL_V, COL_Z, COL_GZ, COL_XS = 0, 1, 2, 3, 4, 5
COL_BC = (3 * GDN_QK + SSD_INNER + GDN_VAL + SSD_INNER) // (2 * SSD_GROUPS * SSD_STATE)

NEG_BIG = -1e30
LOG2_E = 1.4426950408889634


def _dot(a, b):
    return jnp.dot(a, b, preferred_element_type=F32)


def _dot_nt(a, b):
    return lax.dot_general(a, b, (((1,), (1,)), ((), ())), preferred_element_type=F32)


def _dot_tn(a, b):
    return lax.dot_general(a, b, (((0,), (0,)), ((), ())), preferred_element_type=F32)


def _split_hi_lo(v):
    hi = v.astype(BF16)
    lo = (v - hi.astype(F32)).astype(BF16)
    return hi, lo


def _softplus(x):
    return jnp.maximum(x, 0.0) + jnp.log1p(jnp.exp(-jnp.abs(x)))


def _sigmoid(x):
    return 1.0 / (1.0 + jnp.exp(-x))


def _silu(x):
    return x * _sigmoid(x)


def _lower_tri(n, dtype):
    row = lax.broadcasted_iota(jnp.int32, (n, n), 0)
    col = lax.broadcasted_iota(jnp.int32, (n, n), 1)
    return (row >= col).astype(dtype)


def _tile_rows(a8, rows):
    return jnp.concatenate([a8] * (rows // SUBLANES), axis=0)


def _row_to_tile(row, rows):
    return _tile_rows(jnp.broadcast_to(row, (SUBLANES, row.shape[1])), rows)


def _conv_silu(x, tail, w8, b8):
    t, c = x.shape
    tap = lambda k: _tile_rows(w8[k * SUBLANES:(k + 1) * SUBLANES, :], t)
    acc = x * tap(CONV_WIDTH - 1) + _tile_rows(b8, t)
    row8 = lax.broadcasted_iota(jnp.int32, (SUBLANES, c), 0)
    for k in range(1, CONV_WIDTH):
        r = pltpu.roll(x, k, 0)
        tr = pltpu.roll(tail, k, 0)
        first = jnp.where(row8 < k, tr, r[0:SUBLANES])
        shifted = jnp.concatenate([first, r[SUBLANES:]], axis=0)
        acc = acc + shifted * tap(CONV_WIDTH - 1 - k)
    return _silu(acc), x[t - SUBLANES:t]


PROJ_COL_CHUNK = 256
IN_PROJ_COL_CHUNK = 256


def _rmsnorm_rows(x, w):
    ms = jnp.mean(x * x, axis=-1, keepdims=True)
    return x * lax.rsqrt(ms + EPS) * w


GATE_COLS = (COL_Z * GDN_QK, (COL_GZ + 1) * GDN_QK)
RAW_COLS = (COL_XS * SSD_INNER, (COL_XS + 1) * SSD_INNER)


def _in_proj_kernel(tiles_per_seq, x_ref, nw_ref, w_ref, ws_ref, cw_ref, cb_ref, main_ref, small_ref,
                    tail_ref, win_ref):
    tm = x_ref.shape[0]

    @pl.when(pl.program_id(0) % tiles_per_seq == 0)
    def _init():
        tail_ref[...] = jnp.zeros_like(tail_ref)

    h = _rmsnorm_rows(x_ref[...], nw_ref[...]).astype(BF16)
    small_ref[...] = _dot(h, ws_ref[...])
    for c in range(MAIN_WIDTH // IN_PROJ_COL_CHUNK):
        sl = slice(c * IN_PROJ_COL_CHUNK, (c + 1) * IN_PROJ_COL_CHUNK)
        y = _dot(h, w_ref[:, sl])
        if GATE_COLS[0] <= sl.start < GATE_COLS[1]:
            main_ref[:, sl] = _silu(y)
            continue
        if RAW_COLS[0] <= sl.start < RAW_COLS[1]:
            main_ref[:, sl] = y
            continue
        rep = lambda a: jnp.concatenate([a] * (tm // SUBLANES), axis=0)
        w0, w1, w2, w3 = (rep(cw_ref[k * SUBLANES:(k + 1) * SUBLANES, sl]) for k in range(CONV_WIDTH))
        bias = rep(cb_ref[:, sl])
        win_y = win_ref.at[0, c % 2]
        win_u = win_ref.at[1, c % 2]
        win_y[0:SUBLANES, :] = tail_ref[0, :, sl]
        win_y[SUBLANES:, :] = y
        tail_ref[0, :, sl] = y[tm - SUBLANES:tm]
        y1 = win_y[pl.ds(SUBLANES - 1, tm), :]
        u = y * w1 + y1 * w0
        win_u[0:SUBLANES, :] = tail_ref[1, :, sl]
        win_u[SUBLANES:, :] = u
        tail_ref[1, :, sl] = u[tm - SUBLANES:tm]
        acc = y * w3 + y1 * w2 + win_u[pl.ds(SUBLANES - 2, tm), :] + bias
        main_ref[:, sl] = _silu(acc)


def _const_spec(shape):
    nd = len(shape)
    return pl.BlockSpec(shape, lambda *_: (0,) * nd, pipeline_mode=pl.Buffered(1))


def _in_proj(x2, seqlen, norm_w, w_main, w_small, conv_w, conv_b, tm=256):
    m = x2.shape[0]
    return pl.pallas_call(
        functools.partial(_in_proj_kernel, seqlen // tm),
        grid=(m // tm,),
        in_specs=[
            pl.BlockSpec((tm, D_MODEL), lambda i: (i, 0)),
            _const_spec((1, D_MODEL)),
            _const_spec((D_MODEL, MAIN_WIDTH)),
            _const_spec((D_MODEL, LANES)),
            _const_spec((CONV_WIDTH * SUBLANES, MAIN_WIDTH)),
            _const_spec((SUBLANES, MAIN_WIDTH)),
        ],
        out_specs=[
            pl.BlockSpec((tm, MAIN_WIDTH), lambda i: (i, 0)),
            pl.BlockSpec((tm, LANES), lambda i: (i, 0)),
        ],
        out_shape=[
            jax.ShapeDtypeStruct((m, MAIN_WIDTH), F32),
            jax.ShapeDtypeStruct((m, LANES), F32),
        ],
        scratch_shapes=[pltpu.VMEM((2, SUBLANES, MAIN_WIDTH), F32),
                        pltpu.VMEM((2, 2, SUBLANES + tm, IN_PROJ_COL_CHUNK), F32)],
        compiler_params=pltpu.CompilerParams(
            dimension_semantics=("arbitrary",), vmem_limit_bytes=VMEM_LIMIT_BYTES),
        name="in_proj",
    )(x2, norm_w, w_main, w_small, conv_w, conv_b)


def _qkv_proj_kernel(x_ref, nw_ref, w_ref, gmat_ref, qw_ref, kw_ref, o_ref):
    h = _rmsnorm_rows(x_ref[...], nw_ref[...]).astype(BF16)
    gmat = gmat_ref[...]
    n_chunks = D_MODEL // PROJ_COL_CHUNK
    qk_chunks = [(part * D_MODEL + c * PROJ_COL_CHUNK, hw_ref)
                 for part, hw_ref in ((0, qw_ref), (1, kw_ref)) for c in range(n_chunks)]

    def finish(col, hw_ref, y):
        ms = _dot((y * y).astype(BF16), gmat) * (1.0 / SB_HEAD_DIM)
        o_ref[:, col:col + PROJ_COL_CHUNK] = (y * lax.rsqrt(ms + EPS) * hw_ref[...]).astype(o_ref.dtype)

    pending = None
    for col, hw_ref in qk_chunks:
        y = _dot(h, w_ref[:, col:col + PROJ_COL_CHUNK])
        if pending is not None:
            finish(*pending)
        pending = (col, hw_ref, y)
    for c in range(n_chunks):
        col = 2 * D_MODEL + c * PROJ_COL_CHUNK
        v = _dot(h, w_ref[:, col:col + PROJ_COL_CHUNK])
        if pending is not None:
            finish(*pending)
            pending = None
        o_ref[:, col:col + PROJ_COL_CHUNK] = v.astype(o_ref.dtype)


def _qkv_proj(x2, norm_w, w_qkv, gmat, qw, kw, tm=512):
    m = x2.shape[0]
    return pl.pallas_call(
        _qkv_proj_kernel,
        grid=(m // tm,),
        in_specs=[
            pl.BlockSpec((tm, D_MODEL), lambda i: (i, 0)),
            _const_spec((1, D_MODEL)),
            _const_spec((D_MODEL, 3 * D_MODEL)),
            _const_spec((PROJ_COL_CHUNK, PROJ_COL_CHUNK)),
            _const_spec((1, PROJ_COL_CHUNK)),
            _const_spec((1, PROJ_COL_CHUNK)),
        ],
        out_specs=pl.BlockSpec((tm, 3 * D_MODEL), lambda i: (i, 0)),
        out_shape=jax.ShapeDtypeStruct((m, 3 * D_MODEL), BF16),
        compiler_params=pltpu.CompilerParams(
            dimension_semantics=("parallel",), vmem_limit_bytes=VMEM_LIMIT_BYTES),
        name="qkv_proj",
    )(x2, norm_w, w_qkv, gmat, qw, kw)


SSD_STEP_CHUNKS = 4


def _ssd_kernel(gate_ref, xs_ref, bc_ref, sm_ref, cwx_ref, cbx_ref, dtb_ref, alog_ref, dsk_ref, nw_ref, e_ref,
                y_ref, tailx_ref, state_ref):
    t = SSD_CHUNK
    n = SSD_STATE
    hpg = SSD_HEADS // SSD_GROUPS
    gw = hpg * SSD_HEAD_DIM
    chunks = range(SSD_STEP_CHUNKS)
    rows = [slice(c * t, (c + 1) * t) for c in chunks]
    groups = range(SSD_GROUPS)

    @pl.when(pl.program_id(1) == 0)
    def _init():
        tailx_ref[...] = jnp.zeros_like(tailx_ref)
        state_ref[...] = jnp.zeros_like(state_ref)

    xs_all, tailx_ref[...] = _conv_silu(xs_ref[...], tailx_ref[...], cwx_ref[...], cbx_ref[...])
    bc_b = bc_ref[...].astype(BF16)
    gate = gate_ref[...]
    dsk = _tile_rows(dsk_ref[...], t)
    nw = _tile_rows(nw_ref[...], t)

    lane = lax.broadcasted_iota(jnp.int32, (1, LANES), 1)
    head_lane = (lane >= SM_DT) & (lane < SM_DT + SSD_HEADS)
    dt_all = jnp.where(head_lane, _softplus(sm_ref[...] + dtb_ref[...]), 0.0)
    a_all = dt_all * (-jnp.exp(alog_ref[...]))
    tri = _lower_tri(t, BF16)
    a_cum, dt = [], []
    for c in chunks:
        a_hi, a_lo = _split_hi_lo(a_all[rows[c]])
        a_cum.append(_dot(tri, a_hi) + _dot(tri, a_lo))
        dt.append(dt_all[rows[c]])
    a_cum_t = [a_cum[c].T for c in chunks]
    dt_t = [dt[c].T for c in chunks]
    e_a = [jnp.exp(a_cum[c]) for c in chunks]
    w_state = [dt[c] * jnp.exp(a_cum[c][t - 1:t, :] - a_cum[c]) for c in chunks]

    e_mat = e_ref[...]
    ex = []
    for c in chunks:
        ex_hi, ex_lo = _split_hi_lo(jnp.concatenate([w_state[c], e_a[c]], axis=0))
        ex.append(_dot(ex_hi, e_mat) + _dot(ex_lo, e_mat))
    w_state_x = [ex[c][:t] for c in chunks]
    e_a_x = [ex[c][t:] for c in chunks]

    xs = [xs_all[rows[c]] for c in chunks]
    xw = [(xs[c] * w_state_x[c]).astype(BF16) for c in chunks]
    b_g = [[bc_b[rows[c], g * n:(g + 1) * n] for g in groups] for c in chunks]
    c_g = [[bc_b[rows[c], (SSD_GROUPS + g) * n:(SSD_GROUPS + g + 1) * n] for g in groups] for c in chunks]
    cb = [[_dot_nt(c_g[c][g], b_g[c][g]) for g in groups] for c in chunks]
    upd = [jnp.concatenate([_dot_tn(b_g[c][g], xw[c][:, g * gw:(g + 1) * gw]) for g in groups], axis=1)
           for c in chunks]

    state = state_ref[...]
    y_off = []
    for c in chunks:
        state_b = state.astype(BF16)
        y_off.append(jnp.concatenate([_dot(c_g[c][g], state_b[:, g * gw:(g + 1) * gw]) for g in groups], axis=1)
                     * e_a_x[c])
        state = state * _row_to_tile(e_a_x[c][t - 1:t, :], n) + upd[c]
    state_ref[...] = state

    row = lax.broadcasted_iota(jnp.int32, (t, t), 0)
    col = lax.broadcasted_iota(jnp.int32, (t, t), 1)
    causal = row >= col
    xs_b = [xs[c].astype(BF16) for c in chunks]
    y_heads = [[None] * SSD_HEADS for _ in chunks]
    for h in range(SSD_HEADS):
        g = h // hpg
        for c in chunks:
            a_col = jnp.broadcast_to(a_cum[c][:, h:h + 1], (t, t))
            seg = jnp.where(causal, a_col - _row_to_tile(a_cum_t[c][h:h + 1, :], t), NEG_BIG)
            m = (cb[c][g] * jnp.exp(seg) * _row_to_tile(dt_t[c][h:h + 1, :], t)).astype(BF16)
            y_heads[c][h] = _dot(m, xs_b[c][:, h * SSD_HEAD_DIM:(h + 1) * SSD_HEAD_DIM])

    outs = []
    for c in chunks:
        y = jnp.concatenate(y_heads[c], axis=1) + y_off[c] + xs[c] * dsk
        y = y * gate[rows[c]]
        parts = []
        for g in groups:
            yg = y[:, g * gw:(g + 1) * gw]
            ms = jnp.mean(yg * yg, axis=-1, keepdims=True)
            parts.append(yg * lax.rsqrt(ms + EPS))
        outs.append((jnp.concatenate(parts, axis=1) * nw).astype(y_ref.dtype))
    y_ref[...] = jnp.concatenate(outs, axis=0)


def _ssd_mixer(main, small, bsz, seqlen, cwx, cbx, dtb, alog, dsk, nw, e_mat):
    t = SSD_STEP_CHUNKS * SSD_CHUNK
    nc = seqlen // t
    bc_w = 2 * SSD_GROUPS * SSD_STATE
    return pl.pallas_call(
        _ssd_kernel,
        grid=(bsz, nc),
        in_specs=[
            pl.BlockSpec((t, SSD_INNER), lambda b, c: (b * nc + c, COL_Z)),
            pl.BlockSpec((t, SSD_INNER), lambda b, c: (b * nc + c, COL_XS)),
            pl.BlockSpec((t, bc_w), lambda b, c: (b * nc + c, COL_BC)),
            pl.BlockSpec((t, LANES), lambda b, c: (b * nc + c, 0)),
            _const_spec((CONV_WIDTH * SUBLANES, SSD_INNER)),
            _const_spec((SUBLANES, SSD_INNER)),
            _const_spec((1, LANES)),
            _const_spec((1, LANES)),
            _const_spec((SUBLANES, SSD_INNER)),
            _const_spec((SUBLANES, SSD_INNER)),
            _const_spec((LANES, SSD_INNER)),
        ],
        out_specs=pl.BlockSpec((t, SSD_INNER), lambda b, c: (b * nc + c, 0)),
        out_shape=jax.ShapeDtypeStruct((bsz * seqlen, SSD_INNER), BF16),
        scratch_shapes=[pltpu.VMEM((SUBLANES, SSD_INNER), F32), pltpu.VMEM((SSD_STATE, SSD_INNER), F32)],
        compiler_params=pltpu.CompilerParams(
            dimension_semantics=("parallel", "arbitrary"), vmem_limit_bytes=VMEM_LIMIT_BYTES),
        name="ssd_mixer",
    )(main, main, main, small, cwx, cbx, dtb, alog, dsk, nw, e_mat)


def _l2norm_rows(x):
    return x * lax.rsqrt(jnp.sum(x * x, axis=-1, keepdims=True) + EPS)


GDN_STEP_CHUNKS = 8
GDN_PAIR = 2
GDN_QUAD = 4


def _pair_block_diag(m):
    lane = lax.broadcasted_iota(jnp.int32, m.shape, 1)
    zero = jnp.zeros((), m.dtype)
    return jnp.concatenate([jnp.where(lane < LANES, m, zero), jnp.where(lane >= LANES, m, zero)], axis=0)


def _packed_rows(rows8):
    n_pairs = GDN_HEADS // 2
    head = lax.broadcasted_iota(jnp.int32, rows8.shape, 0)
    m = jnp.where((head & 1) == 1, pltpu.roll(rows8, GDN_CHUNK, 1), rows8)
    m4 = jnp.concatenate([m] * n_pairs, axis=1)
    head4 = lax.broadcasted_iota(jnp.int32, m4.shape, 0)
    pair_of_lane = lax.broadcasted_iota(jnp.int32, m4.shape, 1) // LANES
    y = jnp.where((head4 >> 1) == pair_of_lane, m4, 0.0)
    for shift in (4, 2, 1):
        y = y + pltpu.roll(y, shift, 0)
    return jnp.concatenate([y] * (GDN_CHUNK // SUBLANES), axis=0)


def _gdn_kernel(q_ref, k_ref, v_ref, gate_ref, sm_ref, dtb_ref, alog_ref, nw_ref, e64_ref,
                o_ref, s_ref):
    t = GDN_CHUNK

    @pl.when(pl.program_id(1) == 0)
    def _init():
        s_ref[...] = jnp.zeros_like(s_ref)

    q = q_ref[...]
    k = k_ref[...]
    v_b = v_ref[...].astype(BF16)
    sm = sm_ref[...]
    gate = gate_ref[...]
    nw = nw_ref[...]
    e64 = e64_ref[...]
    nc = GDN_STEP_CHUNKS
    chunks = range(nc)
    pairs = range(GDN_HEADS // GDN_PAIR)
    quads = range(GDN_HEADS // GDN_QUAD)
    rows = [slice(c * t, (c + 1) * t) for c in chunks]
    pw = GDN_HEADS * t
    pair_w = GDN_PAIR * GDN_DK
    pair_pw = GDN_PAIR * t
    psl = [slice(p * pair_w, (p + 1) * pair_w) for p in pairs]
    ppl = [slice(p * pair_pw, (p + 1) * pair_pw) for p in pairs]

    qn, kn = [], []
    for h in range(GDN_HEADS):
        sl = slice(h * GDN_DK, (h + 1) * GDN_DK)
        qn.append(_l2norm_rows(q[:, sl]) * (GDN_DK ** -0.5))
        kn.append(_l2norm_rows(k[:, sl]))
    q_b = jnp.concatenate(qn, axis=1).astype(BF16)
    k_b = jnp.concatenate(kn, axis=1).astype(BF16)

    lane = lax.broadcasted_iota(jnp.int32, (1, LANES), 1)
    alpha_lane = (lane >= SM_ALPHA) & (lane < SM_ALPHA + GDN_HEADS)
    beta_lane = (lane >= SM_BETA) & (lane < SM_BETA + GDN_HEADS)
    beta = jnp.where(beta_lane, _sigmoid(sm), 0.0)
    g = jnp.where(alpha_lane, -jnp.exp(alog_ref[...]) * _softplus(sm + dtb_ref[...]), 0.0)
    row_s = lax.broadcasted_iota(jnp.int32, (nc * t, nc * t), 0)
    col_s = lax.broadcasted_iota(jnp.int32, (nc * t, nc * t), 1)
    tri = ((row_s >= col_s) & ((row_s // t) == (col_s // t))).astype(BF16)
    g_hi, g_lo = _split_hi_lo(g)
    gc_all = _dot(tri, g_hi) + _dot(tri, g_lo)
    e_gc_all = jnp.exp(gc_all)
    gc = [gc_all[r] for r in rows]
    e_gc = [e_gc_all[r] for r in rows]
    e_k = [jnp.exp(gc[c][t - 1:t, :] - gc[c]) for c in chunks]

    pad = jnp.zeros((LANES - t, LANES), F32)
    gc_rows8 = [jnp.concatenate([gc[c], pad], axis=0).T[SM_ALPHA:SM_ALPHA + GDN_HEADS, :] for c in chunks]
    beta_rows8 = [jnp.concatenate([beta[rows[c]], pad], axis=0).T[SM_BETA:SM_BETA + GDN_HEADS, :] for c in chunks]
    gc_rp = [_packed_rows(gc_rows8[c]) for c in chunks]
    beta_rp = [_packed_rows(beta_rows8[c]) for c in chunks]
    be_rp = [_packed_rows(beta_rows8[c] * jnp.exp(gc_rows8[c])) for c in chunks]
    gc_hi, gc_lo = _split_hi_lo(gc_all)
    ex = _dot(jnp.concatenate([gc_hi, gc_lo, beta.astype(BF16)], axis=0), e64)
    gc_cp_all = ex[:nc * t] + ex[nc * t:2 * nc * t]
    beta_cp_all = ex[2 * nc * t:]

    rowp = lax.broadcasted_iota(jnp.int32, (t, pw), 0)
    colp = lax.broadcasted_iota(jnp.int32, (t, pw), 1) & (t - 1)
    causal_p = colp <= rowp
    strict_p = colp < rowp
    decay_p = [jnp.exp(jnp.where(causal_p, gc_cp_all[rows[c]] - gc_rp[c], NEG_BIG)) for c in chunks]

    kq = [[_dot_nt(jnp.concatenate([k_b[rows[c], psl[p]], q_b[rows[c], psl[p]]], axis=0),
                   _pair_block_diag(k_b[rows[c], psl[p]])) for p in pairs] for c in chunks]
    amat = [jnp.where(strict_p, jnp.concatenate([kq[c][p][:t] for p in pairs], axis=1)
                      * beta_cp_all[rows[c]] * decay_p[c], 0.0) for c in chunks]
    qk_b = [jnp.where(causal_p, jnp.concatenate([kq[c][p][t:] for p in pairs], axis=1) * decay_p[c], 0.0).astype(BF16)
            for c in chunks]

    qd = GDN_QUAD * t
    blk_r = lax.broadcasted_iota(jnp.int32, (qd, qd), 0) // t
    blk_c = lax.broadcasted_iota(jnp.int32, (qd, qd), 1) // t
    same_block = blk_r == blk_c
    eye_q = ((lax.broadcasted_iota(jnp.int32, (t, qd), 1) & (t - 1))
             == lax.broadcasted_iota(jnp.int32, (t, qd), 0)).astype(F32)

    def block_diag4(m):
        return jnp.where(same_block, jnp.concatenate([m] * GDN_QUAD, axis=0), 0.0).astype(BF16)

    units = [(c, d) for c in chunks for d in quads]
    a_q = {u: amat[u[0]][:, u[1] * qd:(u[1] + 1) * qd] for u in units}
    x_q = {u: eye_q - a_q[u] for u in units}
    p_q = {u: _dot(a_q[u].astype(BF16), block_diag4(a_q[u])) for u in units}
    n_sq = (t - 1).bit_length() - 1
    for it in range(n_sq):
        last = it + 1 == n_sq
        xp = {u: _dot((x_q[u] if last else jnp.concatenate([x_q[u], p_q[u]], axis=0)).astype(BF16),
                      block_diag4(p_q[u])) for u in units}
        x_q = {u: x_q[u] + xp[u][:t] for u in units}
        if not last:
            p_q = {u: xp[u][t:] for u in units}
    t_inv = [jnp.concatenate([x_q[(c, d)] for d in quads], axis=1) for c in chunks]
    tb = [(t_inv[c] * beta_rp[c]).astype(BF16) for c in chunks]
    tbe = [(t_inv[c] * be_rp[c]).astype(BF16) for c in chunks]

    u_base = [jnp.concatenate([_dot(tb[c][:, ppl[p]], _pair_block_diag(v_b[rows[c], psl[p]])) for p in pairs], axis=1)
              for c in chunks]
    w_b = [[_dot(tbe[c][:, ppl[p]], _pair_block_diag(k_b[rows[c], psl[p]])).astype(BF16) for p in pairs]
           for c in chunks]

    def expand_cols(x, base):
        return jnp.concatenate(
            [jnp.broadcast_to(x[:, base + h:base + h + 1], (t, GDN_DV)) for h in range(GDN_HEADS)], axis=1)

    e_gc_x = [expand_cols(e_gc[c], SM_ALPHA) for c in chunks]
    e_k_x = [expand_cols(e_k[c], SM_ALPHA) for c in chunks]

    s_all = s_ref[...]
    o_chunks = []
    for c in chunks:
        s_b = s_all.astype(BF16)
        r = [_dot(jnp.concatenate([w_b[c][p], q_b[rows[c], psl[p]]], axis=0), _pair_block_diag(s_b[:, psl[p]]))
             for p in pairs]
        u = u_base[c] - jnp.concatenate([r[p][:t] for p in pairs], axis=1)
        u_b = u.astype(BF16)
        u_dec_b = (u * e_k_x[c]).astype(BF16)
        o2 = [_dot(qk_b[c][:, ppl[p]], _pair_block_diag(u_b[:, psl[p]])) for p in pairs]
        full = [_dot_tn(k_b[rows[c], psl[p]], u_dec_b[:, psl[p]]) for p in pairs]
        upd = jnp.concatenate([blk for p in pairs for blk in (full[p][:GDN_DK, :GDN_DV], full[p][GDN_DK:, GDN_DV:])],
                              axis=1)
        s_all = s_all * e_gc_x[c][t - 1:t, :] + upd
        o_chunks.append(jnp.concatenate([r[p][t:] for p in pairs], axis=1) * e_gc_x[c] + jnp.concatenate(o2, axis=1))
    s_ref[...] = s_all

    o = jnp.concatenate(o_chunks, axis=0)
    outs = []
    for h in range(GDN_HEADS):
        sl = slice(h * GDN_DV, (h + 1) * GDN_DV)
        oh = o[:, sl]
        oh = oh * lax.rsqrt(jnp.mean(oh * oh, axis=-1, keepdims=True) + EPS) * nw
        outs.append((oh * gate[:, sl]).astype(o_ref.dtype))
    o_ref[...] = jnp.concatenate(outs, axis=1)


def _gdn_mixer(main, small, bsz, seqlen, dtb, alog, nw, e64):
    t = GDN_STEP_CHUNKS * GDN_CHUNK
    nc = seqlen // t
    row_spec = lambda colblk: pl.BlockSpec((t, GDN_QK), lambda b, c: (b * nc + c, colblk))
    return pl.pallas_call(
        _gdn_kernel,
        grid=(bsz, nc),
        in_specs=[
            row_spec(COL_Q), row_spec(COL_K), row_spec(COL_V), row_spec(COL_GZ),
            pl.BlockSpec((t, LANES), lambda b, c: (b * nc + c, 0)),
            _const_spec((1, LANES)),
            _const_spec((1, LANES)),
            _const_spec((1, GDN_DV)),
            _const_spec((LANES, GDN_HEADS * GDN_CHUNK)),
        ],
        out_specs=pl.BlockSpec((t, GDN_VAL), lambda b, c: (b * nc + c, 0)),
        out_shape=jax.ShapeDtypeStruct((bsz * seqlen, GDN_VAL), BF16),
        scratch_shapes=[pltpu.VMEM((GDN_DK, GDN_HEADS * GDN_DV), F32)],
        compiler_params=pltpu.CompilerParams(
            dimension_semantics=("parallel", "arbitrary"), vmem_limit_bytes=VMEM_LIMIT_BYTES),
        name="gdn_mixer",
    )(main, main, main, main, small, dtb, alog, nw, e64)


SB_TILE = 256
SB_LANE_TILES = 8
SB_HEADS_PER_LANE_TILE = LANES // SB_HEAD_DIM


def _sb_kernel(q_ref, k_ref, v_ref, su_ref, o_ref, acc_ref, carry_ref):
    t = SB_TILE
    qi = pl.program_id(2)
    lane = lax.broadcasted_iota(jnp.int32, (1, LANES), 1)
    head_masks = [(lane // SB_HEAD_DIM) == p for p in range(SB_HEADS_PER_LANE_TILE)]
    zero_b = jnp.zeros((), BF16)
    q = q_ref[...]
    qm = [[jnp.where(hm, q[:, g * LANES:(g + 1) * LANES], zero_b) for hm in head_masks]
          for g in range(SB_LANE_TILES)]
    row = lax.broadcasted_iota(jnp.int32, (t, t), 0)
    col = lax.broadcasted_iota(jnp.int32, (t, t), 1)
    valid = col < row
    su = su_ref[...]

    acc_ref[...] = jnp.zeros_like(acc_ref)
    carry_ref[...] = jnp.zeros_like(carry_ref)

    def key_tile(j, diagonal):
        start = pl.multiple_of(j * t, t)
        k_all = k_ref[pl.ds(start, t), :]
        v_all = v_ref[pl.ds(start, t), :]
        carry_all = carry_ref[...]
        heads = [(g, p) for g in range(SB_LANE_TILES) for p in range(SB_HEADS_PER_LANE_TILE)]
        lanes_of = lambda g: slice(g * LANES, (g + 1) * LANES)
        s = [_dot_nt(qm[g][p], k_all[:, lanes_of(g)]) for g, p in heads]
        log_beta, log_1m = [], []
        for i in range(len(heads)):
            sb = s[i].astype(BF16)
            soft = jnp.log(1.0 + jnp.exp2(jnp.abs(sb) * (-LOG2_E)))
            lb = jnp.minimum(sb, 0.0) - soft
            lm = lb - sb
            if diagonal:
                lm = jnp.where(valid, lm, jnp.zeros((), BF16))
            log_beta.append(lb)
            log_1m.append(lm)
        suffix = [_dot(lm, su) for lm in log_1m]
        parts, carries = [], []
        for i, (g, p) in enumerate(heads):
            carry = carry_all[:, i * LANES:(i + 1) * LANES]
            att = jnp.exp(log_beta[i].astype(F32) + suffix[i] + jnp.concatenate([carry] * (t // LANES), axis=1))
            if diagonal:
                att = jnp.where(valid, att, 0.0)
            parts.append(_dot(att.astype(BF16), jnp.where(head_masks[p], v_all[:, lanes_of(g)], zero_b)))
            carries.append(carry + jnp.sum(log_1m[i].astype(F32), axis=-1, keepdims=True))
        outs = [functools.reduce(lambda a, b: a + b, parts[g * SB_HEADS_PER_LANE_TILE:(g + 1) * SB_HEADS_PER_LANE_TILE])
                for g in range(SB_LANE_TILES)]
        acc_ref[...] += jnp.concatenate(outs, axis=1)
        carry_ref[...] = jnp.concatenate(carries, axis=1)

    key_tile(qi, True)

    def body(i, c):
        key_tile(qi - 1 - i, False)
        return c

    lax.fori_loop(0, qi, body, 0)
    o_ref[...] = acc_ref[...].astype(o_ref.dtype)


def _sb_attention(qkv, bsz, seqlen, su):
    t = SB_TILE
    nq = seqlen // t
    w = SB_LANE_TILES * LANES
    n_groups = D_MODEL // w
    return pl.pallas_call(
        _sb_kernel,
        grid=(bsz, n_groups, nq),
        in_specs=[
            pl.BlockSpec((t, w), lambda b, p, i: (b * nq + i, p)),
            pl.BlockSpec((seqlen, w), lambda b, p, i: (b, n_groups + p)),
            pl.BlockSpec((seqlen, w), lambda b, p, i: (b, 2 * n_groups + p)),
            _const_spec((t, t)),
        ],
        out_specs=pl.BlockSpec((t, w), lambda b, p, i: (b * nq + i, p)),
        out_shape=jax.ShapeDtypeStruct((bsz * seqlen, D_MODEL), BF16),
        scratch_shapes=[
            pltpu.VMEM((t, w), F32),
            pltpu.VMEM((t, SB_LANE_TILES * SB_HEADS_PER_LANE_TILE * LANES), F32),
        ],
        compiler_params=pltpu.CompilerParams(
            dimension_semantics=("parallel", "parallel", "arbitrary"), vmem_limit_bytes=VMEM_LIMIT_BYTES),
        name="sb_attention",
    )(qkv, qkv, qkv, su)


MLP_FF_CHUNK = 512


def _post_kernel(n_y, x_ref, *refs):
    y_refs = refs[:n_y]
    wo_refs = refs[n_y:2 * n_y]
    nw_ref, w1_ref, w2_ref, o_ref, hid_ref = refs[2 * n_y:]
    x1 = x_ref[...]
    for y_ref, wo_ref in zip(y_refs, wo_refs):
        x1 = x1 + _dot(y_ref[...], wo_ref[...])
    h = _rmsnorm_rows(x1, nw_ref[...]).astype(BF16)
    for c in range(D_FF // MLP_FF_CHUNK):
        sl = slice(c * MLP_FF_CHUNK, (c + 1) * MLP_FF_CHUNK)
        a = jnp.maximum(_dot(h, w1_ref[:, sl]), 0.0)
        hid_ref[:, sl] = (a * a).astype(BF16)
    o_ref[...] = x1 + _dot(hid_ref[...], w2_ref[...])


def _post_block(x2, ys, wos, norm_w, w1, w2, tm=512):
    m = x2.shape[0]
    n_y = len(ys)
    row = lambda width: pl.BlockSpec((tm, width), lambda i: (i, 0))
    return pl.pallas_call(
        functools.partial(_post_kernel, n_y),
        grid=(m // tm,),
        in_specs=[row(D_MODEL)] + [row(y.shape[1]) for y in ys]
        + [_const_spec(w.shape) for w in wos]
        + [_const_spec((1, D_MODEL)), _const_spec((D_MODEL, D_FF)), _const_spec((D_FF, D_MODEL))],
        out_specs=row(D_MODEL),
        out_shape=jax.ShapeDtypeStruct((m, D_MODEL), F32),
        scratch_shapes=[pltpu.VMEM((tm, D_FF), BF16)],
        compiler_params=pltpu.CompilerParams(
            dimension_semantics=("parallel",), vmem_limit_bytes=VMEM_LIMIT_BYTES),
        name="post_block",
    )(x2, *ys, *wos, norm_w, w1, w2)


def _pad_lanes(vec, offset):
    out = jnp.zeros((1, LANES), F32)
    return out.at[0, offset:offset + vec.shape[0]].set(vec.astype(F32))


def _layer0_params(a_norm_w, a_w_in, ssd_conv_w, ssd_conv_b, ssd_dt_bias, ssd_a_log, ssd_d_skip,
                   ssd_norm_w, gdn_conv_w, gdn_a_log, gdn_dt_bias, gdn_norm_w):
    o_z = 0
    o_xbc = o_z + SSD_INNER
    o_dt = o_xbc + SSD_INNER + 2 * SSD_GROUPS * SSD_STATE
    o_qkv = o_dt + SSD_HEADS
    o_gz = o_qkv + 2 * GDN_QK + GDN_VAL
    o_beta = o_gz + GDN_VAL
    o_alpha = o_beta + GDN_HEADS
    w = a_w_in
    w_main = jnp.concatenate(
        [w[:, o_qkv:o_gz], w[:, o_z:o_xbc], w[:, o_gz:o_beta], w[:, o_xbc:o_dt]], axis=1).astype(BF16)
    w_small = jnp.concatenate(
        [w[:, o_dt:o_qkv], w[:, o_beta:o_alpha], w[:, o_alpha:o_alpha + GDN_HEADS],
         jnp.zeros((D_MODEL, LANES - SSD_HEADS - 2 * GDN_HEADS), w.dtype)], axis=1).astype(BF16)
    head_of_lane = jnp.arange(SSD_INNER) // SSD_HEAD_DIM
    e_mat = (jnp.arange(LANES)[:, None] == head_of_lane[None, :]).astype(BF16)
    head_of_packed = jnp.arange(GDN_HEADS * GDN_CHUNK) // GDN_CHUNK
    src = jnp.arange(LANES)[:, None]
    e64 = ((src == head_of_packed[None, :] + SM_BETA) | (src == head_of_packed[None, :] + SM_ALPHA)).astype(BF16)
    gate_w = GATE_COLS[1] - GATE_COLS[0]
    conv_w = jnp.concatenate([gdn_conv_w.astype(F32), jnp.zeros((CONV_WIDTH, gate_w), F32),
                              ssd_conv_w.astype(F32)], axis=1)
    conv_b = jnp.concatenate([jnp.zeros((GATE_COLS[1],), F32), ssd_conv_b.astype(F32)]).reshape(1, -1)
    conv_w = jnp.repeat(conv_w, SUBLANES, axis=0)
    conv_b = jnp.repeat(conv_b, SUBLANES, axis=0)
    return dict(
        norm_w=a_norm_w.reshape(1, D_MODEL), w_main=w_main, w_small=w_small, conv_w=conv_w, conv_b=conv_b,
        cwx=conv_w[:, RAW_COLS[0]:RAW_COLS[1]], cbx=conv_b[:, RAW_COLS[0]:RAW_COLS[1]],
        ssd_dtb=_pad_lanes(ssd_dt_bias, SM_DT), ssd_alog=_pad_lanes(ssd_a_log, SM_DT),
        dsk=jnp.tile(jnp.repeat(ssd_d_skip.astype(F32), SSD_HEAD_DIM).reshape(1, -1), (SUBLANES, 1)),
        ssd_nw=jnp.tile(ssd_norm_w.astype(F32).reshape(1, -1), (SUBLANES, 1)), e_mat=e_mat,
        gdn_dtb=_pad_lanes(gdn_dt_bias, SM_ALPHA), gdn_alog=_pad_lanes(gdn_a_log, SM_ALPHA),
        gdn_nw=gdn_norm_w.reshape(1, -1), e64=e64)


def kernel(x, a_norm_w, a_w_in, ssd_conv_w, ssd_conv_b, ssd_dt_bias, ssd_a_log, ssd_d_skip, ssd_norm_w,
           gdn_conv_w, gdn_a_log, gdn_dt_bias, gdn_norm_w, a_w_out, c_norm_w, c_w_qkv, c_q_norm_w,
           c_k_norm_w, c_w_o, mlp_norm_w, mlp_w1, mlp_w2):
    bsz, seqlen, d = x.shape
    x2 = x.reshape(bsz * seqlen, d)

    p = _layer0_params(a_norm_w[0], a_w_in[0], ssd_conv_w[0], ssd_conv_b[0], ssd_dt_bias[0], ssd_a_log[0],
                       ssd_d_skip[0], ssd_norm_w[0], gdn_conv_w[0], gdn_a_log[0], gdn_dt_bias[0], gdn_norm_w[0])
    main, small = _in_proj(x2, seqlen, p["norm_w"], p["w_main"], p["w_small"], p["conv_w"], p["conv_b"])
    y_ssd = _ssd_mixer(main, small, bsz, seqlen, p["cwx"], p["cbx"], p["ssd_dtb"], p["ssd_alog"], p["dsk"],
                       p["ssd_nw"], p["e_mat"])
    y_gdn = _gdn_mixer(main, small, bsz, seqlen, p["gdn_dtb"], p["gdn_alog"], p["gdn_nw"], p["e64"])
    w_out = a_w_out[0].astype(BF16)
    x2 = _post_block(x2, [y_ssd, y_gdn], [w_out[:SSD_INNER], w_out[SSD_INNER:]],
                     mlp_norm_w[0].reshape(1, d), mlp_w1[0].astype(BF16), mlp_w2[0].astype(BF16))

    heads_per_chunk = PROJ_COL_CHUNK // SB_HEAD_DIM
    lane_head = jnp.arange(PROJ_COL_CHUNK) // SB_HEAD_DIM
    gmat = (lane_head[:, None] == lane_head[None, :]).astype(BF16)
    qw = jnp.tile(c_q_norm_w[0].astype(F32) * (SB_HEAD_DIM ** -0.5), heads_per_chunk).reshape(1, -1)
    kw = jnp.tile(c_k_norm_w[0].astype(F32), heads_per_chunk).reshape(1, -1)
    qkv = _qkv_proj(x2, c_norm_w[0].reshape(1, d), c_w_qkv[0].astype(BF16), gmat, qw, kw)
    key = jnp.arange(SB_TILE)
    su = (key[:, None] > key[None, :]).astype(BF16)
    o = _sb_attention(qkv, bsz, seqlen, su)
    x2 = _post_block(x2, [o], [c_w_o[0].astype(BF16)],
                     mlp_norm_w[1].reshape(1, d), mlp_w1[1].astype(BF16), mlp_w2[1].astype(BF16))
    return x2.reshape(bsz, seqlen, d)
```
